```python
import math
import jax, jax.numpy as jnp
from jax import lax
import numpy as np

D_MODEL = 1024
BATCH = 8
SEQ = 2048
DEPTH = 4
DEC_BATCH = 128
DEC_SEQ = 4
PAST_LEN = 16384
PAGE_SIZE = 128

POOL_WINDOWS = (2, 4, 8, 16)
N_POOL_GROUPS = 4
POOL_GROUP_DIM = D_MODEL // 8
POOL_DIM = N_POOL_GROUPS * POOL_GROUP_DIM
POOL_BUF = 15
SSM_INNER = D_MODEL
SSM_HEAD_DIM = 64
SSM_HEADS = SSM_INNER // SSM_HEAD_DIM
SSM_GROUPS = 4
SSM_HEADS_PER_GROUP = SSM_HEADS // SSM_GROUPS
SSM_STATE = 128
CONV_WIDTH = 4
CONV_DIM = SSM_INNER + 2 * SSM_GROUPS * SSM_STATE
SSD_CHUNK = 128
N_MEM = 256
MEM_HEADS = 4
MEM_HEAD_DIM = D_MODEL // 8
MEM_DIM = MEM_HEADS * MEM_HEAD_DIM
N_BRANCH = 3
D_FF = 2816
EPS = 1e-6
IN_SPLITS = (POOL_DIM, SSM_INNER, CONV_DIM, SSM_HEADS, MEM_DIM, N_BRANCH * D_MODEL)
IN_COLS = 7184
F32 = jnp.float32

kernel_name = 'hybrid_pool_ssd_memxattn_macaron_step'


def rmsnorm(x, g):
    xf = x.astype(F32)
    y = xf * lax.rsqrt(jnp.mean(xf * xf, axis=-1, keepdims=True) + EPS)
    return (y * g.astype(F32)).astype(x.dtype)


def swiglu(h, w_gate, w_up, w_down):
    return (jax.nn.silu(h @ w_gate) * (h @ w_up)) @ w_down


def pool_mix(u, prev, start_pos, pool_w, pool_scale):
    b, L, _ = u.shape
    full = jnp.concatenate([prev.astype(u.dtype), u], axis=1)
    cs = jnp.cumsum(full.astype(F32), axis=1)
    cs = jnp.concatenate([jnp.zeros((b, 1, POOL_DIM), F32), cs], axis=1)
    upto = cs[:, POOL_BUF + 1:]
    pos = start_pos + jnp.arange(L)
    means = []
    for gi, w in enumerate(POOL_WINDOWS):
        c = slice(gi * POOL_GROUP_DIM, (gi + 1) * POOL_GROUP_DIM)
        before = cs[:, POOL_BUF + 1 - w: POOL_BUF + 1 - w + L, c]
        cnt = jnp.minimum(pos + 1, w).astype(F32)[None, :, None]
        means.append((upto[..., c] - before) / cnt)
    d = (jnp.concatenate(means, axis=-1) - u.astype(F32)).astype(u.dtype)
    d = d.reshape(b, L, N_POOL_GROUPS, POOL_GROUP_DIM)
    y = jnp.einsum('blgc,gcd->blgd', d, pool_w).reshape(b, L, POOL_DIM)
    return y * pool_scale, full[:, -POOL_BUF:]


def causal_conv(xbc, prev, conv_w, conv_b):
    full = jnp.concatenate([prev.astype(xbc.dtype), xbc], axis=1)
    y = lax.conv_general_dilated(full, conv_w[:, None, :].astype(xbc.dtype), window_strides=(1,),
                                 padding='VALID', dimension_numbers=('NWC', 'WIO', 'NWC'),
                                 feature_group_count=CONV_DIM)
    return jax.nn.silu(y + conv_b), full[:, -(CONV_WIDTH - 1):]


def ssd_scan(x, dt, a, bm, cm, h0):
    b, L = x.shape[0], x.shape[1]
    q = min(SSD_CHUNK, L)
    nc = -(-L // q)
    pad = nc * q - L

    def tpad(t):
        return jnp.pad(t, [(0, 0), (0, pad)] + [(0, 0)] * (t.ndim - 2))

    x, dt, bm, cm = tpad(x.astype(F32)), tpad(dt), tpad(bm.astype(F32)), tpad(cm.astype(F32))
    G, HG, P, N = SSM_GROUPS, SSM_HEADS_PER_GROUP, SSM_HEAD_DIM, SSM_STATE
    xdt = (x * dt[..., None]).reshape(b, nc, q, G, HG, P)
    la = (dt * a).reshape(b, nc, q, G, HG).transpose(0, 1, 3, 4, 2)
    acum = jnp.cumsum(la, axis=-1)
    bm = bm.reshape(b, nc, q, G, N)
    cm = cm.reshape(b, nc, q, G, N)
    causal = jnp.tril(jnp.ones((q, q), bool))
    seg = acum[..., :, None] - acum[..., None, :]
    decay = jnp.exp(jnp.where(causal, seg, -jnp.inf))
    cb = jnp.einsum('bctgn,bcsgn->bcgts', cm, bm)
    y_diag = jnp.einsum('bcghts,bcsghp->bctghp', cb[:, :, :, None] * decay, xdt)
    to_end = jnp.exp(acum[..., -1:] - acum).transpose(0, 1, 4, 2, 3)
    chunk_states = jnp.einsum('bcsgn,bcsghp->bcghpn', bm, xdt * to_end[..., None])
    chunk_decay = jnp.exp(acum[..., -1])

    def step(h, inp):
        st, dc = inp
        return h * dc[..., None, None] + st, h

    h_last, h_in = lax.scan(step, h0.reshape(b, G, HG, P, N),
                            (jnp.moveaxis(chunk_states, 1, 0), jnp.moveaxis(chunk_decay, 1, 0)))
    h_in = jnp.moveaxis(h_in, 0, 1)
    from_start = jnp.exp(acum).transpose(0, 1, 4, 2, 3)
    y_off = jnp.einsum('bctgn,bcghpn->bctghp', cm, h_in) * from_start[..., None]
    y = (y_diag + y_off).reshape(b, nc * q, SSM_HEADS, P)[:, :L]
    return y, h_last.reshape(b, SSM_HEADS, P, N)


def mem_attend(q, k, v):
    s = jnp.einsum('blhd,bmhd->bhlm', q, k.astype(q.dtype)).astype(F32) * (MEM_HEAD_DIM ** -0.5)
    w = jax.nn.softmax(s, axis=-1).astype(v.dtype)
    return jnp.einsum('bhlm,bmhd->blhd', w, v)


def mixer(h, start_pos, pool_prev, conv_prev, ssm_h0, mem_k, mem_v, p):
    b, L, _ = h.shape
    cuts = [int(c) for c in np.cumsum(IN_SPLITS)[:-1]]
    u_pool, z, xbc, dt_raw, q_mem, g_raw = jnp.split(h @ p['w_in'], cuts, axis=-1)
    y_pool, new_pool = pool_mix(u_pool, pool_prev, start_pos, p['pool_w'], p['pool_scale'])
    xbc, new_conv = causal_conv(xbc, conv_prev, p['conv_w'], p['conv_b'])
    xs, bm, cm = jnp.split(xbc, [SSM_INNER, SSM_INNER + SSM_GROUPS * SSM_STATE], axis=-1)
    xs = xs.reshape(b, L, SSM_HEADS, SSM_HEAD_DIM)
    dt = jax.nn.softplus(dt_raw.astype(F32) + p['dt_bias'].astype(F32))
    a = -jnp.exp(p['a_log'].astype(F32))
    y, new_ssm = ssd_scan(xs, dt, a, bm.reshape(b, L, SSM_GROUPS, SSM_STATE),
                          cm.reshape(b, L, SSM_GROUPS, SSM_STATE), ssm_h0.astype(F32))
    y = (y + p['d_skip'].astype(F32)[:, None] * xs.astype(F32)).reshape(b, L, SSM_INNER)
    yg = (y * jax.nn.silu(z.astype(F32))).reshape(b, L, SSM_GROUPS, SSM_INNER // SSM_GROUPS)
    yg = yg * lax.rsqrt(jnp.mean(yg * yg, axis=-1, keepdims=True) + EPS)
    y_ssm = (yg.reshape(b, L, SSM_INNER) * p['ssm_norm'].astype(F32)).astype(h.dtype)
    y_mem = mem_attend(q_mem.reshape(b, L, MEM_HEADS, MEM_HEAD_DIM), mem_k, mem_v).reshape(b, L, MEM_DIM)
    gates = jax.nn.sigmoid(g_raw.astype(F32) + p['gate_bias'].astype(F32)).reshape(b, L, N_BRANCH, D_MODEL)
    merged = (gates[:, :, 0] * (y_pool @ p['w_br_pool']).astype(F32)
              + gates[:, :, 1] * (y_ssm @ p['w_br_ssm']).astype(F32)
              + gates[:, :, 2] * (y_mem.astype(h.dtype) @ p['w_br_mem']).astype(F32))
    return merged.astype(h.dtype) @ p['w_o'], new_pool, new_conv, new_ssm.astype(ssm_h0.dtype)


def trunk_layer(x, start_pos, pool_prev, conv_prev, ssm_h0, mem_k, mem_v, p):
    x = x + 0.5 * swiglu(rmsnorm(x, p['ffn1_norm']), p['ffn1_w_gate'], p['ffn1_w_up'], p['ffn1_w_down'])
    mix, new_pool, new_conv, new_ssm = mixer(rmsnorm(x, p['mix_norm']), start_pos, pool_prev, conv_prev,
                                             ssm_h0, mem_k, mem_v, p)
    x = x + mix
    x = x + 0.5 * swiglu(rmsnorm(x, p['ffn2_norm']), p['ffn2_w_gate'], p['ffn2_w_up'], p['ffn2_w_down'])
    return x, new_pool, new_conv, new_ssm


def setup_inputs(seed: int = 0) -> dict:
    key = jax.random.key(seed)
    ks = jax.random.split(key, 40)

    def nrm(i, shape, scale):
        return scale * jax.random.normal(ks[i], shape, F32)

    def gain(i, shape):
        return 1.0 + nrm(i, shape, 0.02)

    Lr, D = DEPTH, D_MODEL
    dt0 = jnp.exp(jax.random.uniform(ks[20], (Lr, SSM_HEADS), F32, math.log(1e-3), math.log(1e-1)))
    dt_bias = dt0 + jnp.log(-jnp.expm1(-dt0))
    a_log = jnp.log(jax.random.uniform(ks[21], (Lr, SSM_HEADS), F32, 1.0, 16.0))
    return {
        'x_prompt': nrm(0, (BATCH, SEQ, D), 1.0),
        'x_sample': nrm(1, (DEC_BATCH, DEC_SEQ, D), 1.0),
        'mem_prompt': nrm(2, (BATCH, N_MEM, D), 1.0),
        'state_pool': nrm(3, (DEPTH, DEC_BATCH, POOL_BUF, POOL_DIM), 1.0),
        'state_conv': nrm(4, (DEPTH, DEC_BATCH, CONV_WIDTH - 1, CONV_DIM), 1.0),
        'state_ssm': nrm(5, (DEPTH, DEC_BATCH, SSM_HEADS, SSM_HEAD_DIM, SSM_STATE), 0.3),
        'cache_mem_k': nrm(6, (DEPTH, DEC_BATCH, N_MEM, MEM_HEADS, MEM_HEAD_DIM), 1.0),
        'cache_mem_v': nrm(7, (DEPTH, DEC_BATCH, N_MEM, MEM_HEADS, MEM_HEAD_DIM), 1.0),
        'ffn1_norm': gain(8, (Lr, D)),
        'ffn1_w_gate': nrm(9, (Lr, D, D_FF), D ** -0.5),
        'ffn1_w_up': nrm(10, (Lr, D, D_FF), D ** -0.5),
        'ffn1_w_down': nrm(11, (Lr, D_FF, D), D_FF ** -0.5),
        'mix_norm': gain(12, (Lr, D)),
        'w_in': nrm(13, (Lr, D, IN_COLS), D ** -0.5),
        'gate_bias': nrm(14, (Lr, N_BRANCH * D), 0.02),
        'pool_w': nrm(15, (Lr, N_POOL_GROUPS, POOL_GROUP_DIM, POOL_GROUP_DIM), POOL_GROUP_DIM ** -0.5),
        'pool_scale': 1.0 + nrm(16, (Lr, POOL_DIM), 0.1),
        'conv_w': nrm(17, (Lr, CONV_WIDTH, CONV_DIM), CONV_WIDTH ** -0.5),
        'conv_b': nrm(18, (Lr, CONV_DIM), 0.02),
        'dt_bias': dt_bias,
        'a_log': a_log,
        'd_skip': 1.0 + nrm(22, (Lr, SSM_HEADS), 0.1),
        'ssm_norm': gain(23, (Lr, SSM_INNER)),
        'mem_norm': gain(24, (Lr, D)),
        'w_mem_k': nrm(25, (Lr, D, MEM_DIM), D ** -0.5),
        'w_mem_v': nrm(26, (Lr, D, MEM_DIM), D ** -0.5),
        'w_br_pool': nrm(27, (Lr, POOL_DIM, D), POOL_DIM ** -0.5),
        'w_br_ssm': nrm(28, (Lr, SSM_INNER, D), SSM_INNER ** -0.5),
        'w_br_mem': nrm(29, (Lr, MEM_DIM, D), MEM_DIM ** -0.5),
        'w_o': nrm(30, (Lr, D, D), D ** -0.5),
        'ffn2_norm': gain(31, (Lr, D)),
        'ffn2_w_gate': nrm(32, (Lr, D, D_FF), D ** -0.5),
        'ffn2_w_up': nrm(33, (Lr, D, D_FF), D ** -0.5),
        'ffn2_w_down': nrm(34, (Lr, D_FF, D), D_FF ** -0.5),
        'final_norm': gain(35, (D,)),
    }


def reference(x_prompt, x_sample, mem_prompt, state_pool, state_conv, state_ssm, cache_mem_k, cache_mem_v,
              ffn1_norm, ffn1_w_gate, ffn1_w_up, ffn1_w_down, mix_norm, w_in, gate_bias,
              pool_w, pool_scale, conv_w, conv_b, dt_bias, a_log, d_skip, ssm_norm,
              mem_norm, w_mem_k, w_mem_v, w_br_pool, w_br_ssm, w_br_mem, w_o,
              ffn2_norm, ffn2_w_gate, ffn2_w_up, ffn2_w_down, final_norm):
    bp = x_prompt.shape[0]
    n_mem = mem_prompt.shape[1]
    xp, xs = x_prompt, x_sample
    pool_p, conv_p, ssm_p, mk_p, mv_p = [], [], [], [], []
    pool_s, conv_s, ssm_s = [], [], []
    for i in range(DEPTH):
        p = dict(ffn1_norm=ffn1_norm[i], ffn1_w_gate=ffn1_w_gate[i], ffn1_w_up=ffn1_w_up[i],
                 ffn1_w_down=ffn1_w_down[i], mix_norm=mix_norm[i], w_in=w_in[i], gate_bias=gate_bias[i],
                 pool_w=pool_w[i], pool_scale=pool_scale[i], conv_w=conv_w[i], conv_b=conv_b[i],
                 dt_bias=dt_bias[i], a_log=a_log[i], d_skip=d_skip[i], ssm_norm=ssm_norm[i],
                 w_br_pool=w_br_pool[i], w_br_ssm=w_br_ssm[i], w_br_mem=w_br_mem[i], w_o=w_o[i],
                 ffn2_norm=ffn2_norm[i], ffn2_w_gate=ffn2_w_gate[i], ffn2_w_up=ffn2_w_up[i],
                 ffn2_w_down=ffn2_w_down[i])
        mem_h = rmsnorm(mem_prompt, mem_norm[i])
        mk = (mem_h @ w_mem_k[i]).reshape(bp, n_mem, MEM_HEADS, MEM_HEAD_DIM)
        mv = (mem_h @ w_mem_v[i]).reshape(bp, n_mem, MEM_HEADS, MEM_HEAD_DIM)
        xp, npool, nconv, nssm = trunk_layer(
            xp, 0,
            jnp.zeros((bp, POOL_BUF, POOL_DIM), xp.dtype),
            jnp.zeros((bp, CONV_WIDTH - 1, CONV_DIM), xp.dtype),
            jnp.zeros((bp, SSM_HEADS, SSM_HEAD_DIM, SSM_STATE), F32),
            mk, mv, p)
        pool_p.append(npool)
        conv_p.append(nconv)
        ssm_p.append(nssm)
        mk_p.append(mk)
        mv_p.append(mv)
        xs, npool, nconv, nssm = trunk_layer(
            xs, PAST_LEN, state_pool[i], state_conv[i], state_ssm[i], cache_mem_k[i], cache_mem_v[i], p)
        pool_s.append(npool)
        conv_s.append(nconv)
        ssm_s.append(nssm)
    y_prompt = rmsnorm(xp, final_norm)
    y_sample = rmsnorm(xs, final_norm)
    return (y_prompt, y_sample,
            jnp.stack(pool_p), jnp.stack(conv_p), jnp.stack(ssm_p), jnp.stack(mk_p), jnp.stack(mv_p),
            jnp.stack(pool_s), jnp.stack(conv_s), jnp.stack(ssm_s))
```

```python
import functools

import jax
import jax.numpy as jnp
from jax import lax
from jax.experimental import pallas as pl
from jax.experimental.pallas import tpu as pltpu

F32 = jnp.float32
BF16 = jnp.bfloat16

D_MODEL = 1024
D_FF = 2816
EPS = 1e-6
PAST_LEN = 16384

POOL_WINDOWS = (2, 4, 8, 16)
POOL_GROUP_DIM = 128
POOL_DIM = 512
POOL_BUF = 15
SSM_INNER = 1024
SSM_HEADS = 16
SSM_HEAD_DIM = 64
SSM_GROUPS = 4
SSM_HEADS_PER_GROUP = 4
SSM_GROUP_DIM = 256
SSM_STATE = 128
CONV_WIDTH = 4
CONV_DIM = 2048
MEM_HEADS = 4
MEM_HEAD_DIM = 128
MEM_DIM = 512
N_BRANCH = 3
IN_SPLITS = (POOL_DIM, SSM_INNER, CONV_DIM, SSM_HEADS, MEM_DIM, N_BRANCH * D_MODEL)

SUBLANES = 8
LANES = 128

FFN_TOKEN_TILE = 512
FFN_FF_CHUNK = 1408
PROMPT_CHUNK = 256
SAMPLE_SEQ_TILE = 16
SAMPLE_ATTN_TILE = 8
SEQ_ROWS = 8
POOL_SEQ_ROWS = 24
VMEM_LIMIT = 56 * 1024 * 1024

_NT = (((1,), (1,)), ((), ()))
_TN = (((0,), (0,)), ((), ()))


def _dot(a, b):
    return jnp.dot(a, b, preferred_element_type=F32)


def _dot_nt(a, b):
    return lax.dot_general(a, b, _NT, preferred_element_type=F32)


def _rmsnorm(x, g):
    return x * lax.rsqrt(jnp.mean(x * x, axis=-1, keepdims=True) + EPS) * g


def _silu(x):
    return x * jax.nn.sigmoid(x)


def _softplus(x):
    return jnp.maximum(x, 0.0) + jnp.log1p(jnp.exp(-jnp.abs(x)))


def _const_spec(shape, index, single_buffer=True):
    mode = pl.Buffered(1) if single_buffer else None
    return pl.BlockSpec(shape, lambda *_: index, pipeline_mode=mode)


def _expand_group(v, g, lane):
    h0 = SSM_HEADS_PER_GROUP * g
    out = jnp.broadcast_to(v[:, h0 + 3:h0 + 4], lane.shape)
    for j in (2, 1, 0):
        out = jnp.where(lane < SSM_HEAD_DIM * (j + 1), v[:, h0 + j:h0 + j + 1], out)
    return out


def _expand_heads(v, lane):
    return jnp.concatenate([_expand_group(v, g, lane) for g in range(SSM_GROUPS)], axis=1)


def _ffn_body(x, g, wg_ref, wu_ref, wd_ref):
    hn = _rmsnorm(x, g).astype(BF16)
    acc = None
    for j in range(D_FF // FFN_FF_CHUNK):
        sl = slice(j * FFN_FF_CHUNK, (j + 1) * FFN_FF_CHUNK)
        gate = _dot(hn, wg_ref[:, sl])
        up = _dot(hn, wu_ref[:, sl])
        part = _dot((_silu(gate) * up).astype(BF16), wd_ref[sl, :])
        acc = part if acc is None else acc + part
    return x + 0.5 * acc


def _ffn_kernel(x_ref, g_ref, wg_ref, wu_ref, wd_ref, o_ref):
    o_ref[...] = _ffn_body(x_ref[...], g_ref[...], wg_ref, wu_ref, wd_ref)


def _ffn_final_kernel(x_ref, g_ref, wg_ref, wu_ref, wd_ref, fg_ref, o_ref):
    o_ref[...] = _rmsnorm(_ffn_body(x_ref[...], g_ref[...], wg_ref, wu_ref, wd_ref), fg_ref[...])


def _ffn_call(x, layer, norm, w_gate, w_up, w_down, final_norm=None):
    n_tok = x.shape[0]
    tile = pl.BlockSpec((FFN_TOKEN_TILE, D_MODEL), lambda i: (i, 0))
    in_specs = [tile,
                _const_spec((None, 1, D_MODEL), (layer, 0, 0)),
                _const_spec((None, D_MODEL, D_FF), (layer, 0, 0)),
                _const_spec((None, D_MODEL, D_FF), (layer, 0, 0)),
                _const_spec((None, D_FF, D_MODEL), (layer, 0, 0))]
    args = [x, norm, w_gate, w_up, w_down]
    kern = _ffn_kernel
    if final_norm is not None:
        in_specs.append(_const_spec((1, D_MODEL), (0, 0)))
        args.append(final_norm)
        kern = _ffn_final_kernel
    return pl.pallas_call(
        kern,
        out_shape=jax.ShapeDtypeStruct(x.shape, F32),
        grid=(n_tok // FFN_TOKEN_TILE,),
        in_specs=in_specs,
        out_specs=tile,
        input_output_aliases={0: 0},
        compiler_params=pltpu.CompilerParams(dimension_semantics=("arbitrary",),
                                             vmem_limit_bytes=VMEM_LIMIT),
        name="ffn_final" if final_norm is not None else "ffn",
    )(*args)


def _memkv_kernel(m_ref, g_ref, wk_ref, wv_ref, k_ref, v_ref, kb_ref, vb_ref):
    hn = _rmsnorm(m_ref[...], g_ref[...]).astype(BF16)
    k = _dot(hn, wk_ref[...])
    v = _dot(hn, wv_ref[...])
    k_ref[...] = k
    v_ref[...] = v
    kb_ref[...] = k.astype(BF16)
    vb_ref[...] = v.astype(BF16)


def _memkv_call(mem, norm, wk, wv):
    depth = norm.shape[0]
    batch, n_mem, _ = mem.shape
    out_spec = pl.BlockSpec((None, None, n_mem, MEM_DIM), lambda l, b: (l, b, 0, 0))
    w_spec = pl.BlockSpec((None, D_MODEL, MEM_DIM), lambda l, b: (l, 0, 0))
    f32_out = jax.ShapeDtypeStruct((depth, batch, n_mem, MEM_DIM), F32)
    bf16_out = jax.ShapeDtypeStruct((depth, batch, n_mem, MEM_DIM), BF16)
    return pl.pallas_call(
        _memkv_kernel,
        out_shape=(f32_out, f32_out, bf16_out, bf16_out),
        grid=(depth, batch),
        in_specs=[pl.BlockSpec((None, n_mem, D_MODEL), lambda l, b: (b, 0, 0)),
                  pl.BlockSpec((None, 1, D_MODEL), lambda l, b: (l, 0, 0)),
                  w_spec, w_spec],
        out_specs=(out_spec, out_spec, out_spec, out_spec),
        compiler_params=pltpu.CompilerParams(dimension_semantics=("arbitrary", "arbitrary")),
        name="mem_kv",
    )(mem, norm, wk, wv)


def _merge_out(x, z, graw, y_raw, y_pool, y_mem, gbias, snorm, wbp_ref, wbs_ref, wbm_ref, wo_ref):
    yz = y_raw * _silu(z)
    parts = []
    for g in range(SSM_GROUPS):
        yg = yz[:, g * SSM_GROUP_DIM:(g + 1) * SSM_GROUP_DIM]
        parts.append(yg * lax.rsqrt(jnp.mean(yg * yg, axis=-1, keepdims=True) + EPS))
    y_ssm = (jnp.concatenate(parts, axis=1) * snorm).astype(BF16)
    gates = jax.nn.sigmoid(graw + gbias)
    merged = (gates[:, :D_MODEL] * _dot(y_pool.astype(BF16), wbp_ref[...])
              + gates[:, D_MODEL:2 * D_MODEL] * _dot(y_ssm, wbs_ref[...])
              + gates[:, 2 * D_MODEL:] * _dot(y_mem.astype(BF16), wbm_ref[...]))
    return x + _dot(merged.astype(BF16), wo_ref[...])


def _attend(q, k, v):
    s = _dot_nt(q, k) * (MEM_HEAD_DIM ** -0.5)
    e = jnp.exp(s - jnp.max(s, axis=-1, keepdims=True))
    p = e * (1.0 / jnp.sum(e, axis=-1, keepdims=True))
    return _dot(p.astype(BF16), v)


def _prompt_mixer_kernel(x_ref, norm_ref, wpool_ref, wz_ref, wxbc_ref, wdt_ref, wq_ref, wg_ref,
                         gbias_ref, poolw_ref, pscale_ref, convw_ref, convb_ref, dtb_ref, alog_ref,
                         dskip_ref, snorm_ref, wbp_ref, wbs_ref, wbm_ref, wo_ref, k_ref, v_ref,
                         xo_ref, poolst_ref, convst_ref, ssmst_ref,
                         h_ref, cbuf_ref, pbuf_ref):
    T = PROMPT_CHUNK
    c = pl.program_id(1)
    last = pl.num_programs(1) - 1

    @pl.when(c == 0)
    def _():
        h_ref[...] = jnp.zeros_like(h_ref)
        cbuf_ref[0:SUBLANES, :] = jnp.zeros((SUBLANES, CONV_DIM), F32)
        pbuf_ref[0:2 * SUBLANES, :] = jnp.zeros((2 * SUBLANES, POOL_DIM), F32)

    x = x_ref[...]
    hn = _rmsnorm(x, norm_ref[...]).astype(BF16)

    P0 = 2 * SUBLANES
    u = _dot(hn, wpool_ref[...])
    pbuf_ref[P0:P0 + T, :] = u
    pos = c * T + lax.broadcasted_iota(jnp.int32, (T, 1), 0)
    y_parts = []
    for g, w in enumerate(POOL_WINDOWS):
        cs = slice(g * POOL_GROUP_DIM, (g + 1) * POOL_GROUP_DIM)
        ug = u[:, cs]
        s = ug
        for k in range(1, w):
            s = s + pbuf_ref[P0 - k:P0 - k + T, cs]
        inv_cnt = 1.0 / jnp.minimum(pos + 1, w).astype(F32)
        d = (s * inv_cnt - ug).astype(BF16)
        y_parts.append(_dot(d, poolw_ref[g]))
    y_pool = jnp.concatenate(y_parts, axis=1) * pscale_ref[...]

    @pl.when(c == last)
    def _():
        poolst_ref[...] = pbuf_ref[P0 + T - POOL_BUF:P0 + T, :]

    pbuf_ref[0:P0, :] = pbuf_ref[T:T + P0, :]

    C0 = SUBLANES
    xbc = _dot(hn, wxbc_ref[...])
    cbuf_ref[C0:C0 + T, :] = xbc
    acc = convb_ref[...] + convw_ref[CONV_WIDTH - 1:CONV_WIDTH, :] * xbc
    for k in range(CONV_WIDTH - 1):
        off = C0 - (CONV_WIDTH - 1) + k
        acc = acc + convw_ref[k:k + 1, :] * cbuf_ref[off:off + T, :]
    xc = _silu(acc)

    @pl.when(c == last)
    def _():
        convst_ref[...] = cbuf_ref[C0 + T - (CONV_WIDTH - 1):C0 + T, :]

    cbuf_ref[0:C0, :] = cbuf_ref[T:T + C0, :]

    xs = xc[:, :SSM_INNER]
    bm = xc[:, SSM_INNER:SSM_INNER + SSM_GROUPS * SSM_STATE]
    cm = xc[:, SSM_INNER + SSM_GROUPS * SSM_STATE:]

    dt = _softplus(_dot(hn, wdt_ref[...]) + dtb_ref[...])
    la = dt * (-jnp.exp(alog_ref[...]))
    la_t = la.T[0:SSM_HEADS, :]
    lane_t = lax.broadcasted_iota(jnp.int32, (SSM_HEADS, T), 1)
    acum_t = la_t
    shift = 1
    while shift < T:
        acum_t = acum_t + jnp.where(lane_t >= shift, pltpu.roll(acum_t, shift, 1), 0.0)
        shift *= 2
    acum = jnp.concatenate([acum_t, jnp.zeros((LANES - SSM_HEADS, T), F32)], axis=0).T
    from_start = jnp.exp(acum)
    to_end = jnp.exp(acum[T - 1:T, :] - acum)
    chunk_decay = jnp.exp(acum_t[:, T - 1:T])

    lane_g = lax.broadcasted_iota(jnp.int32, (T, SSM_GROUP_DIM), 1)
    causal = (lax.broadcasted_iota(jnp.int32, (T, T), 0) >= lax.broadcasted_iota(jnp.int32, (T, T), 1))
    y_groups = []
    for g in range(SSM_GROUPS):
        cm_g = cm[:, g * SSM_STATE:(g + 1) * SSM_STATE].astype(BF16)
        bm_g = bm[:, g * SSM_STATE:(g + 1) * SSM_STATE].astype(BF16)
        cb = _dot_nt(cm_g, bm_g)
        xdt = xs[:, g * SSM_GROUP_DIM:(g + 1) * SSM_GROUP_DIM] * _expand_group(dt, g, lane_g)
        lhs, rhs = [], []
        for j in range(SSM_HEADS_PER_GROUP):
            hh = SSM_HEADS_PER_GROUP * g + j
            seg = acum[:, hh:hh + 1] - acum_t[hh:hh + 1, :]
            decay = jnp.exp(jnp.where(causal, seg, -jnp.inf))
            lhs.append((cb * decay).astype(BF16))
            in_head = (lane_g >= SSM_HEAD_DIM * j) & (lane_g < SSM_HEAD_DIM * (j + 1))
            rhs.append(jnp.where(in_head, xdt, 0.0).astype(BF16))
        y_diag = _dot(jnp.concatenate(lhs, axis=1), jnp.concatenate(rhs, axis=0))
        rows = slice(g * SSM_GROUP_DIM, (g + 1) * SSM_GROUP_DIM)
        h_prev = h_ref[rows, :]
        y_off = _dot_nt(cm_g, h_prev.astype(BF16)) * _expand_group(from_start, g, lane_g)
        xw = (xdt * _expand_group(to_end, g, lane_g)).astype(BF16)
        st = lax.dot_general(xw, bm_g, _TN, preferred_element_type=F32)
        for j in range(SSM_HEADS_PER_GROUP):
            hh = SSM_HEADS_PER_GROUP * g + j
            r = slice(g * SSM_GROUP_DIM + j * SSM_HEAD_DIM, g * SSM_GROUP_DIM + (j + 1) * SSM_HEAD_DIM)
            h_ref[r, :] = (h_prev[j * SSM_HEAD_DIM:(j + 1) * SSM_HEAD_DIM, :] * chunk_decay[hh:hh + 1, :]
                           + st[j * SSM_HEAD_DIM:(j + 1) * SSM_HEAD_DIM, :])
        y_groups.append(y_diag + y_off)
    y_raw = jnp.concatenate(y_groups, axis=1) + dskip_ref[...] * xs

    @pl.when(c == last)
    def _():
        ssmst_ref[...] = h_ref[...]

    q = _dot(hn, wq_ref[...])
    o = []
    for h in range(MEM_HEADS):
        hs = slice(h * MEM_HEAD_DIM, (h + 1) * MEM_HEAD_DIM)
        o.append(_attend(q[:, hs].astype(BF16), k_ref[:, hs], v_ref[:, hs]))
    y_mem = jnp.concatenate(o, axis=1)

    z = _dot(hn, wz_ref[...])
    graw = _dot(hn, wg_ref[...])
    xo_ref[...] = _merge_out(x, z, graw, y_raw, y_pool, y_mem, gbias_ref[...], snorm_ref[...],
                             wbp_ref, wbs_ref, wbm_ref, wo_ref)


def _prompt_mixer_call(x, layer, batch, seq, p, kb, vb):
    T = PROMPT_CHUNK
    n_chunks = seq // T
    n_mem = kb.shape[2]
    x_spec = pl.BlockSpec((T, D_MODEL), lambda b, c: (b * n_chunks + c, 0))

    def wspec(rows, cols):
        return _const_spec((None, rows, cols), (layer, 0, 0))

    def vspec(cols):
        return _const_spec((None, 1, cols), (layer, 0, 0))

    kv_spec = pl.BlockSpec((None, None, n_mem, MEM_DIM), lambda b, c: (layer, b, 0, 0))
    in_specs = [x_spec, vspec(D_MODEL),
                wspec(D_MODEL, POOL_DIM), wspec(D_MODEL, SSM_INNER), wspec(D_MODEL, CONV_DIM),
                wspec(D_MODEL, LANES), wspec(D_MODEL, MEM_DIM), wspec(D_MODEL, N_BRANCH * D_MODEL),
                vspec(N_BRANCH * D_MODEL),
                _const_spec((None, 4, POOL_GROUP_DIM, POOL_GROUP_DIM), (layer, 0, 0, 0)),
                vspec(POOL_DIM), wspec(CONV_WIDTH, CONV_DIM), vspec(CONV_DIM), vspec(LANES), vspec(LANES),
                vspec(SSM_INNER), vspec(SSM_INNER),
                wspec(POOL_DIM, D_MODEL), wspec(SSM_INNER, D_MODEL), wspec(MEM_DIM, D_MODEL),
                wspec(D_MODEL, D_MODEL), kv_spec, kv_spec]
    out_shape = (jax.ShapeDtypeStruct(x.shape, F32),
                 jax.ShapeDtypeStruct((batch, POOL_BUF, POOL_DIM), F32),
                 jax.ShapeDtypeStruct((batch, CONV_WIDTH - 1, CONV_DIM), F32),
                 jax.ShapeDtypeStruct((batch, SSM_INNER, SSM_STATE), F32))
    out_specs = (x_spec,
                 pl.BlockSpec((None, POOL_BUF, POOL_DIM), lambda b, c: (b, 0, 0)),
                 pl.BlockSpec((None, CONV_WIDTH - 1, CONV_DIM), lambda b, c: (b, 0, 0)),
                 pl.BlockSpec((None, SSM_INNER, SSM_STATE), lambda b, c: (b, 0, 0)))
    return pl.pallas_call(
        _prompt_mixer_kernel,
        out_shape=out_shape,
        grid=(batch, n_chunks),
        in_specs=in_specs,
        out_specs=out_specs,
        scratch_shapes=[pltpu.VMEM((SSM_INNER, SSM_STATE), F32),
                        pltpu.VMEM((SUBLANES + T, CONV_DIM), F32),
                        pltpu.VMEM((2 * SUBLANES + T, POOL_DIM), F32)],
        input_output_aliases={0: 0},
        compiler_params=pltpu.CompilerParams(dimension_semantics=("arbitrary", "arbitrary"),
                                             vmem_limit_bytes=VMEM_LIMIT),
        name="prompt_mixer",
    )(x, p["mix_norm"], p["w_pool"], p["w_z"], p["w_xbc"], p["w_dt"], p["w_q"], p["w_g"],
      p["gate_bias"], p["pool_w"], p["pool_scale"], p["conv_w"], p["conv_b"], p["dt_bias"], p["a_log"],
      p["d_skip"], p["ssm_norm"], p["w_br_pool"], p["w_br_ssm"], p["w_br_mem"], p["w_o"], kb, vb)


def _sample_inproj_kernel(x_ref, norm_ref, wpool_ref, wz_ref, wxbc_ref, wdt_ref, wq_ref, wg_ref,
                          u_ref, z_ref, xbc_ref, dt_ref, q_ref, g_ref):
    hn = _rmsnorm(x_ref[...], norm_ref[...]).astype(BF16)
    u_ref[...] = _dot(hn, wpool_ref[...])
    z_ref[...] = _dot(hn, wz_ref[...])
    xbc_ref[...] = _dot(hn, wxbc_ref[...])
    dt_ref[...] = _dot(hn, wdt_ref[...])
    q_ref[...] = _dot(hn, wq_ref[...])
    g_ref[...] = _dot(hn, wg_ref[...])


def _sample_inproj_call(x, layer, row_block, n_rows, p):
    def wspec(cols):
        return _const_spec((None, D_MODEL, cols), (layer, 0, 0))

    widths = (POOL_DIM, SSM_INNER, CONV_DIM, LANES, MEM_DIM, N_BRANCH * D_MODEL)
    return pl.pallas_call(
        _sample_inproj_kernel,
        out_shape=tuple(jax.ShapeDtypeStruct((n_rows, w), F32) for w in widths),
        grid=(1,),
        in_specs=[pl.BlockSpec((n_rows, D_MODEL), lambda i: (row_block, 0)),
                  _const_spec((None, 1, D_MODEL), (layer, 0, 0)),
                  wspec(POOL_DIM), wspec(SSM_INNER), wspec(CONV_DIM), wspec(LANES), wspec(MEM_DIM),
                  wspec(N_BRANCH * D_MODEL)],
        out_specs=tuple(pl.BlockSpec((n_rows, w), lambda i: (0, 0)) for w in widths),
        compiler_params=pltpu.CompilerParams(dimension_semantics=("arbitrary",),
                                             vmem_limit_bytes=VMEM_LIMIT),
        name="sample_inproj",
    )(x, p["mix_norm"], p["w_pool"], p["w_z"], p["w_xbc"], p["w_dt"], p["w_q"], p["w_g"])


def _sample_ssd_kernel(cext_ref, pext_ref, dtext_ref, st_ref, convw_ref, convb_ref, dtb_ref, alog_ref,
                       dskip_ref, poolw_ref, pscale_ref,
                       y_ref, ypool_ref, stn_ref, yoff_ref):
    R = SAMPLE_SEQ_TILE * SEQ_ROWS
    n_tok = SEQ_ROWS // 2
    row = lax.broadcasted_iota(jnp.int32, (R, 1), 0)
    t_idx = row % SEQ_ROWS - n_tok
    tok = t_idx >= 0

    f = pext_ref[...]
    sums = []
    s = f
    for step in (1, 2, 4, 8):
        s = s + pltpu.roll(s, step, 0)
        sums.append(s)
    y_parts = []
    for g, w in enumerate(POOL_WINDOWS):
        cs = slice(g * POOL_GROUP_DIM, (g + 1) * POOL_GROUP_DIM)
        d = (sums[g][:, cs] * (1.0 / w) - f[:, cs]).astype(BF16)
        y_parts.append(_dot(d, poolw_ref[g]))
    ypool_ref[...] = jnp.concatenate(y_parts, axis=1) * pscale_ref[...]

    ext = cext_ref[...]
    acc = convb_ref[...] + convw_ref[CONV_WIDTH - 1:CONV_WIDTH, :] * ext
    for k in range(CONV_WIDTH - 1):
        acc = acc + convw_ref[k:k + 1, :] * pltpu.roll(ext, CONV_WIDTH - 1 - k, 0)
    xc = jnp.where(tok, _silu(acc), 0.0)
    xs = xc[:, :SSM_INNER]
    bm = xc[:, SSM_INNER:SSM_INNER + SSM_GROUPS * SSM_STATE]
    cm = xc[:, SSM_INNER + SSM_GROUPS * SSM_STATE:]

    dt = jnp.where(tok, _softplus(dtext_ref[...] + dtb_ref[...]), 0.0)
    la = dt * (-jnp.exp(alog_ref[...]))
    c1 = la + pltpu.roll(la, 1, 0)
    acum = c1 + pltpu.roll(c1, 2, 0)
    nxt = pltpu.roll(la, R - 1, 0)
    n1 = nxt + pltpu.roll(nxt, R - 1, 0)
    rest = n1 + pltpu.roll(n1, R - 2, 0)
    lane_g = lax.broadcasted_iota(jnp.int32, (R, SSM_GROUP_DIM), 1)
    lane_h = lax.broadcasted_iota(jnp.int32, (R, LANES), 1)

    xdt = xs * _expand_heads(dt, lane_g)
    xdt_q = xdt.astype(BF16).astype(F32)
    cm_q = cm.astype(BF16).astype(F32)
    bm_q = bm.astype(BF16).astype(F32)

    y_diag = jnp.zeros((R, SSM_INNER), F32)
    for k in range(n_tok):
        b_k = bm_q if k == 0 else pltpu.roll(bm_q, k, 0)
        prod = cm_q * b_k
        cb = jnp.zeros((R, LANES), F32)
        for g in reversed(range(SSM_GROUPS)):
            s_g = jnp.sum(prod[:, g * SSM_STATE:(g + 1) * SSM_STATE], axis=-1, keepdims=True)
            cb = jnp.where(lane_h < SSM_HEADS_PER_GROUP * (g + 1), s_g, cb)
        a_k = acum if k == 0 else pltpu.roll(acum, k, 0)
        decay = jnp.exp(jnp.where(t_idx >= k, acum - a_k, -jnp.inf))
        coef = (cb * decay).astype(BF16).astype(F32)
        x_k = xdt_q if k == 0 else pltpu.roll(xdt_q, k, 0)
        y_diag = y_diag + _expand_heads(coef, lane_g) * x_k

    xw_t = (xdt * _expand_heads(jnp.exp(rest), lane_g)).T
    total_t = jnp.exp(acum + rest).T
    bm_b = bm.astype(BF16)
    col = lax.broadcasted_iota(jnp.int32, (SSM_GROUP_DIM, R), 1)
    for b in range(SAMPLE_SEQ_TILE):
        r0 = b * SEQ_ROWS
        c_rows = cm[r0:r0 + SEQ_ROWS, :]
        in_seq = (col >= r0) & (col < r0 + SEQ_ROWS)
        last_col = r0 + SEQ_ROWS - 1
        for g in range(SSM_GROUPS):
            rows = slice(g * SSM_GROUP_DIM, (g + 1) * SSM_GROUP_DIM)
            h0 = st_ref[b, rows, :]
            c_g = c_rows[:, g * SSM_STATE:(g + 1) * SSM_STATE].astype(BF16)
            yoff_ref[r0:r0 + SEQ_ROWS, rows] = _dot_nt(c_g, h0.astype(BF16))
            lhs = jnp.where(in_seq, xw_t[rows, :], 0.0).astype(BF16)
            st = _dot(lhs, bm_b[:, g * SSM_STATE:(g + 1) * SSM_STATE])
            for j in range(SSM_HEADS_PER_GROUP):
                hh = SSM_HEADS_PER_GROUP * g + j
                hr = slice(j * SSM_HEAD_DIM, (j + 1) * SSM_HEAD_DIM)
                stn_ref[b, g * SSM_GROUP_DIM + j * SSM_HEAD_DIM:g * SSM_GROUP_DIM + (j + 1) * SSM_HEAD_DIM, :] = (
                    h0[hr, :] * total_t[hh:hh + 1, last_col:last_col + 1] + st[hr, :])

    y_ref[...] = (y_diag + yoff_ref[...] * _expand_heads(jnp.exp(acum), lane_g) + dskip_ref[...] * xs)


def _sample_ssd_call(layer, cext, pext, dtext, state, p):
    n_seq = state.shape[0]
    R = SAMPLE_SEQ_TILE * SEQ_ROWS
    RP = SAMPLE_SEQ_TILE * POOL_SEQ_ROWS

    def vspec(rows, cols):
        return _const_spec((None, rows, cols), (layer, 0, 0))

    st_spec = pl.BlockSpec((SAMPLE_SEQ_TILE, SSM_INNER, SSM_STATE), lambda i: (i, 0, 0))
    return pl.pallas_call(
        _sample_ssd_kernel,
        out_shape=(jax.ShapeDtypeStruct((n_seq * SEQ_ROWS, SSM_INNER), F32),
                   jax.ShapeDtypeStruct((n_seq * POOL_SEQ_ROWS, POOL_DIM), F32),
                   jax.ShapeDtypeStruct(state.shape, F32)),
        grid=(n_seq // SAMPLE_SEQ_TILE,),
        in_specs=[pl.BlockSpec((R, CONV_DIM), lambda i: (i, 0)),
                  pl.BlockSpec((RP, POOL_DIM), lambda i: (i, 0)),
                  pl.BlockSpec((R, LANES), lambda i: (i, 0)),
                  st_spec,
                  vspec(CONV_WIDTH, CONV_DIM), vspec(1, CONV_DIM), vspec(1, LANES), vspec(1, LANES),
                  vspec(1, SSM_INNER),
                  _const_spec((None, 4, POOL_GROUP_DIM, POOL_GROUP_DIM), (layer, 0, 0, 0)),
                  vspec(1, POOL_DIM)],
        out_specs=(pl.BlockSpec((R, SSM_INNER), lambda i: (i, 0)),
                   pl.BlockSpec((RP, POOL_DIM), lambda i: (i, 0)),
                   st_spec),
        scratch_shapes=[pltpu.VMEM((R, SSM_INNER), F32)],
        compiler_params=pltpu.CompilerParams(dimension_semantics=("arbitrary",),
                                             vmem_limit_bytes=VMEM_LIMIT),
        name="sample_ssd",
    )(cext, pext, dtext, state, p["conv_w"], p["conv_b"], p["dt_bias"], p["a_log"], p["d_skip"],
      p["pool_w"], p["pool_scale"])


def _sample_attn_kernel(q_ref, k_ref, v_ref, o_ref):
    for b in range(SAMPLE_ATTN_TILE):
        r0 = b * SEQ_ROWS
        q = q_ref[r0:r0 + SEQ_ROWS, :]
        o = []
        for h in range(MEM_HEADS):
            hs = slice(h * MEM_HEAD_DIM, (h + 1) * MEM_HEAD_DIM)
            o.append(_attend(q[:, hs].astype(BF16), k_ref[b, :, hs].astype(BF16), v_ref[b, :, hs].astype(BF16)))
        o_ref[r0:r0 + SEQ_ROWS, :] = jnp.concatenate(o, axis=1)


def _sample_attn_call(layer, qext, cache_k, cache_v):
    n_seq, n_mem = cache_k.shape[1], cache_k.shape[2]
    R = SAMPLE_ATTN_TILE * SEQ_ROWS
    kv_spec = pl.BlockSpec((None, SAMPLE_ATTN_TILE, n_mem, MEM_DIM), lambda i: (layer, i, 0, 0))
    return pl.pallas_call(
        _sample_attn_kernel,
        out_shape=jax.ShapeDtypeStruct((n_seq * SEQ_ROWS, MEM_DIM), F32),
        grid=(n_seq // SAMPLE_ATTN_TILE,),
        in_specs=[pl.BlockSpec((R, MEM_DIM), lambda i: (i, 0)), kv_spec, kv_spec],
        out_specs=pl.BlockSpec((R, MEM_DIM), lambda i: (i, 0)),
        compiler_params=pltpu.CompilerParams(dimension_semantics=("arbitrary",),
                                             vmem_limit_bytes=VMEM_LIMIT),
        name="sample_attn",
    )(qext, cache_k, cache_v)


def _sample_merge_kernel(x_ref, z_ref, g_ref, y_ref, ypool_ref, ymem_ref, gbias_ref, snorm_ref,
                         wbp_ref, wbs_ref, wbm_ref, wo_ref, xo_ref):
    xo_ref[...] = _merge_out(x_ref[...], z_ref[...], g_ref[...], y_ref[...], ypool_ref[...], ymem_ref[...],
                             gbias_ref[...], snorm_ref[...], wbp_ref, wbs_ref, wbm_ref, wo_ref)


def _sample_merge_call(x, layer, row_block, n_rows, z, graw, y_raw, y_pool, y_mem, p):
    def full(cols):
        return pl.BlockSpec((n_rows, cols), lambda i: (0, 0))

    def wspec(rows, cols):
        return _const_spec((None, rows, cols), (layer, 0, 0))

    x_spec = pl.BlockSpec((n_rows, D_MODEL), lambda i: (row_block, 0))
    return pl.pallas_call(
        _sample_merge_kernel,
        out_shape=jax.ShapeDtypeStruct(x.shape, F32),
        grid=(1,),
        in_specs=[x_spec, full(SSM_INNER), full(N_BRANCH * D_MODEL), full(SSM_INNER), full(POOL_DIM),
                  full(MEM_DIM), wspec(1, N_BRANCH * D_MODEL), wspec(1, SSM_INNER),
                  wspec(POOL_DIM, D_MODEL), wspec(SSM_INNER, D_MODEL), wspec(MEM_DIM, D_MODEL),
                  wspec(D_MODEL, D_MODEL)],
        out_specs=x_spec,
        input_output_aliases={0: 0},
        compiler_params=pltpu.CompilerParams(dimension_semantics=("arbitrary",),
                                             vmem_limit_bytes=VMEM_LIMIT),
        name="sample_merge",
    )(x, z, graw, y_raw, y_pool, y_mem, p["gate_bias"], p["ssm_norm"],
      p["w_br_pool"], p["w_br_ssm"], p["w_br_mem"], p["w_o"])


def _row3(a):
    return a.reshape(a.shape[0], 1, a.shape[1])


def _pad_lanes(a):
    return jnp.pad(a, [(0, 0)] * (a.ndim - 1) + [(0, LANES - a.shape[-1])])


def _token_rows(ext, rows_per_seq, n_tok):
    n_seq = ext.shape[0] // rows_per_seq
    return ext.reshape(n_seq, rows_per_seq, -1)[:, rows_per_seq - n_tok:, :].reshape(n_seq * n_tok, -1)


def kernel(x_prompt, x_sample, mem_prompt, state_pool, state_conv, state_ssm, cache_mem_k, cache_mem_v,
           ffn1_norm, ffn1_w_gate, ffn1_w_up, ffn1_w_down, mix_norm, w_in, gate_bias,
           pool_w, pool_scale, conv_w, conv_b, dt_bias, a_log, d_skip, ssm_norm,
           mem_norm, w_mem_k, w_mem_v, w_br_pool, w_br_ssm, w_br_mem, w_o,
           ffn2_norm, ffn2_w_gate, ffn2_w_up, ffn2_w_down, final_norm):
    depth = w_in.shape[0]
    batch, seq, _ = x_prompt.shape
    n_seq, n_tok, _ = x_sample.shape
    n_mem = mem_prompt.shape[1]
    n_prompt = batch * seq
    n_sample = n_seq * n_tok
    assert n_sample == FFN_TOKEN_TILE and n_prompt % FFN_TOKEN_TILE == 0 and seq % PROMPT_CHUNK == 0
    assert n_tok == SEQ_ROWS // 2 and PAST_LEN >= max(POOL_WINDOWS)
    assert n_seq % SAMPLE_SEQ_TILE == 0 and n_seq % SAMPLE_ATTN_TILE == 0
    sample_block = n_prompt // n_sample

    cuts = [0]
    for w in IN_SPLITS:
        cuts.append(cuts[-1] + w)
    w_in_b = w_in.astype(BF16)
    params = dict(
        mix_norm=_row3(mix_norm),
        w_pool=w_in_b[:, :, cuts[0]:cuts[1]], w_z=w_in_b[:, :, cuts[1]:cuts[2]],
        w_xbc=w_in_b[:, :, cuts[2]:cuts[3]], w_dt=_pad_lanes(w_in_b[:, :, cuts[3]:cuts[4]]),
        w_q=w_in_b[:, :, cuts[4]:cuts[5]], w_g=w_in_b[:, :, cuts[5]:cuts[6]],
        gate_bias=_row3(gate_bias), pool_w=pool_w.astype(BF16), pool_scale=_row3(pool_scale),
        conv_w=conv_w, conv_b=_row3(conv_b), dt_bias=_row3(_pad_lanes(dt_bias)), a_log=_row3(_pad_lanes(a_log)),
        d_skip=_row3(jnp.repeat(d_skip, SSM_HEAD_DIM, axis=1)), ssm_norm=_row3(ssm_norm),
        w_br_pool=w_br_pool.astype(BF16), w_br_ssm=w_br_ssm.astype(BF16), w_br_mem=w_br_mem.astype(BF16),
        w_o=w_o.astype(BF16))
    ffn1 = (_row3(ffn1_norm), ffn1_w_gate.astype(BF16), ffn1_w_up.astype(BF16), ffn1_w_down.astype(BF16))
    ffn2 = (_row3(ffn2_norm), ffn2_w_gate.astype(BF16), ffn2_w_up.astype(BF16), ffn2_w_down.astype(BF16))

    mem_k, mem_v, mem_kb, mem_vb = _memkv_call(mem_prompt, _row3(mem_norm), w_mem_k.astype(BF16),
                                               w_mem_v.astype(BF16))
    cache_k = cache_mem_k.reshape(depth, n_seq, n_mem, MEM_DIM)
    cache_v = cache_mem_v.reshape(depth, n_seq, n_mem, MEM_DIM)

    x = jnp.concatenate([x_prompt.reshape(n_prompt, D_MODEL), x_sample.reshape(n_sample, D_MODEL)], axis=0)
    pool_p, conv_p, ssm_p, pool_s, conv_s, ssm_s = [], [], [], [], [], []
    for l in range(depth):
        x = _ffn_call(x, l, *ffn1)
        x, npool, nconv, nssm = _prompt_mixer_call(x, l, batch, seq, params, mem_kb, mem_vb)
        pool_p.append(npool)
        conv_p.append(nconv)
        ssm_p.append(nssm.reshape(batch, SSM_HEADS, SSM_HEAD_DIM, SSM_STATE))

        u, z, xbc, dt_raw, q, graw = _sample_inproj_call(x, l, sample_block, n_sample, params)
        u3 = u.reshape(n_seq, n_tok, POOL_DIM)
        xbc3 = xbc.reshape(n_seq, n_tok, CONV_DIM)
        pext = jnp.concatenate([jnp.zeros((n_seq, POOL_SEQ_ROWS - POOL_BUF - n_tok, POOL_DIM), F32),
                                state_pool[l], u3], axis=1).reshape(n_seq * POOL_SEQ_ROWS, POOL_DIM)
        cext = jnp.concatenate([jnp.zeros((n_seq, SEQ_ROWS - CONV_WIDTH + 1 - n_tok, CONV_DIM), F32),
                                state_conv[l], xbc3], axis=1).reshape(n_seq * SEQ_ROWS, CONV_DIM)

        def pad_seq(a):
            a3 = a.reshape(n_seq, n_tok, a.shape[-1])
            return jnp.concatenate([jnp.zeros_like(a3), a3], axis=1).reshape(n_seq * SEQ_ROWS, a.shape[-1])

        y_ext, ypool_ext, nssm_s = _sample_ssd_call(l, cext, pext, pad_seq(dt_raw),
                                                    state_ssm[l].reshape(n_seq, SSM_INNER, SSM_STATE), params)
        ymem_ext = _sample_attn_call(l, pad_seq(q), cache_k, cache_v)
        x = _sample_merge_call(x, l, sample_block, n_sample, z, graw,
                               _token_rows(y_ext, SEQ_ROWS, n_tok), _token_rows(ypool_ext, POOL_SEQ_ROWS, n_tok),
                               _token_rows(ymem_ext, SEQ_ROWS, n_tok), params)
        pool_s.append(jnp.concatenate([state_pool[l][:, n_tok:], u3], axis=1))
        conv_s.append(xbc3[:, n_tok - (CONV_WIDTH - 1):])
        ssm_s.append(nssm_s.reshape(n_seq, SSM_HEADS, SSM_HEAD_DIM, SSM_STATE))

        x = _ffn_call(x, l, *ffn2, final_norm=final_norm.reshape(1, D_MODEL) if l == depth - 1 else None)

    y_prompt = x[:n_prompt].reshape(batch, seq, D_MODEL)
    y_sample = x[n_prompt:].reshape(n_seq, n_tok, D_MODEL)
    mem_shape = (depth, batch, n_mem, MEM_HEADS, MEM_HEAD_DIM)
    return (y_prompt, y_sample,
            jnp.stack(pool_p), jnp.stack(conv_p), jnp.stack(ssm_p),
            mem_k.reshape(mem_shape), mem_v.reshape(mem_shape),
            jnp.stack(pool_s), jnp.stack(conv_s), jnp.stack(ssm_s))
```

```python
import functools

import jax
import jax.numpy as jnp
from jax import lax
from jax.experimental import pallas as pl
from jax.experimental.pallas import tpu as pltpu

F32 = jnp.float32
BF16 = jnp.bfloat16

D_MODEL = 1024
D_FF = 2816
EPS = 1e-6
PAST_LEN = 16384

POOL_WINDOWS = (2, 4, 8, 16)
POOL_GROUP_DIM = 128
POOL_DIM = 512
POOL_BUF = 15
SSM_INNER = 1024
SSM_HEADS = 16
SSM_HEAD_DIM = 64
SSM_GROUPS = 4
SSM_HEADS_PER_GROUP = 4
SSM_GROUP_DIM = 256
SSM_STATE = 128
CONV_WIDTH = 4
CONV_DIM = 2048
MEM_HEADS = 4
MEM_HEAD_DIM = 128
MEM_DIM = 512
N_BRANCH = 3
IN_SPLITS = (POOL_DIM, SSM_INNER, CONV_DIM, SSM_HEADS, MEM_DIM, N_BRANCH * D_MODEL)

SUBLANES = 8
LANES = 128

FFN_TOKEN_TILE = 512
FFN_FF_CHUNK = 1408
PROMPT_CHUNK = 256
SAMPLE_SEQ_TILE = 16
SAMPLE_ATTN_TILE = 8
SEQ_ROWS = 8
POOL_SEQ_ROWS = 24
VMEM_LIMIT = 56 * 1024 * 1024

_NT = (((1,), (1,)), ((), ()))
_TN = (((0,), (0,)), ((), ()))


def _dot(a, b):
    return jnp.dot(a, b, preferred_element_type=F32)


def _dot_nt(a, b):
    return lax.dot_general(a, b, _NT, preferred_element_type=F32)


def _rmsnorm(x, g):
    return x * lax.rsqrt(jnp.mean(x * x, axis=-1, keepdims=True) + EPS) * g


def _silu(x):
    return x * jax.nn.sigmoid(x)


def _softplus(x):
    return jnp.maximum(x, 0.0) + jnp.log1p(jnp.exp(-jnp.abs(x)))


def _const_spec(shape, index, single_buffer=True):
    mode = pl.Buffered(1) if single_buffer else None
    return pl.BlockSpec(shape, lambda *_: index, pipeline_mode=mode)


def _expand_group(v, g, lane):
    h0 = SSM_HEADS_PER_GROUP * g
    out = jnp.broadcast_to(v[:, h0 + 3:h0 + 4], lane.shape)
    for j in (2, 1, 0):
        out = jnp.where(lane < SSM_HEAD_DIM * (j + 1), v[:, h0 + j:h0 + j + 1], out)
    return out


def _expand_heads(v, lane):
    return jnp.concatenate([_expand_group(v, g, lane) for g in range(SSM_GROUPS)], axis=1)


def _ffn_body(x, g, wg_ref, wu_ref, wd_ref):
    hn = _rmsnorm(x, g).astype(BF16)
    acc = None
    for j in range(D_FF // FFN_FF_CHUNK):
        sl = slice(j * FFN_FF_CHUNK, (j + 1) * FFN_FF_CHUNK)
        gate = _dot(hn, wg_ref[:, sl])
        up = _dot(hn, wu_ref[:, sl])
        part = _dot((_silu(gate) * up).astype(BF16), wd_ref[sl, :])
        acc = part if acc is None else acc + part
    return x + 0.5 * acc


def _ffn_kernel(x_ref, g_ref, wg_ref, wu_ref, wd_ref, o_ref):
    o_ref[...] = _ffn_body(x_ref[...], g_ref[...], wg_ref, wu_ref, wd_ref)


def _ffn_first_kernel(xp_ref, xs_ref, g_ref, wg_ref, wu_ref, wd_ref, o_ref, *, prompt_blocks):
    x = jnp.where(pl.program_id(0) < prompt_blocks, xp_ref[...], xs_ref[...])
    o_ref[...] = _ffn_body(x, g_ref[...], wg_ref, wu_ref, wd_ref)


def _ffn_final_kernel(x_ref, g_ref, wg_ref, wu_ref, wd_ref, fg_ref, yp_ref, ys_ref, *, prompt_blocks):
    y = _rmsnorm(_ffn_body(x_ref[...], g_ref[...], wg_ref, wu_ref, wd_ref), fg_ref[...])
    i = pl.program_id(0)

    @pl.when(i < prompt_blocks)
    def _():
        yp_ref[...] = y

    @pl.when(i == prompt_blocks)
    def _():
        ys_ref[...] = y


def _ffn_weight_specs(layer):
    return [_const_spec((None, 1, D_MODEL), (layer, 0, 0)),
            _const_spec((None, D_MODEL, D_FF), (layer, 0, 0)),
            _const_spec((None, D_MODEL, D_FF), (layer, 0, 0)),
            _const_spec((None, D_FF, D_MODEL), (layer, 0, 0))]


_FFN_PARAMS = pltpu.CompilerParams(dimension_semantics=("arbitrary",), vmem_limit_bytes=VMEM_LIMIT)


def _ffn_call(x, layer, weights):
    tile = pl.BlockSpec((FFN_TOKEN_TILE, D_MODEL), lambda i: (i, 0))
    return pl.pallas_call(
        _ffn_kernel,
        out_shape=jax.ShapeDtypeStruct(x.shape, F32),
        grid=(x.shape[0] // FFN_TOKEN_TILE,),
        in_specs=[tile] + _ffn_weight_specs(layer),
        out_specs=tile,
        input_output_aliases={0: 0},
        compiler_params=_FFN_PARAMS,
        name="ffn",
    )(x, *weights)


def _ffn_first_call(x_prompt, x_sample, layer, weights):
    prompt_blocks = x_prompt.shape[0] // FFN_TOKEN_TILE
    n_tok = x_prompt.shape[0] + x_sample.shape[0]
    tile = pl.BlockSpec((FFN_TOKEN_TILE, D_MODEL), lambda i: (i, 0))
    p_tile = pl.BlockSpec((FFN_TOKEN_TILE, D_MODEL), lambda i: (jnp.minimum(i, prompt_blocks - 1), 0))
    s_tile = pl.BlockSpec((FFN_TOKEN_TILE, D_MODEL), lambda i: (0, 0))
    return pl.pallas_call(
        functools.partial(_ffn_first_kernel, prompt_blocks=prompt_blocks),
        out_shape=jax.ShapeDtypeStruct((n_tok, D_MODEL), F32),
        grid=(n_tok // FFN_TOKEN_TILE,),
        in_specs=[p_tile, s_tile] + _ffn_weight_specs(layer),
        out_specs=tile,
        compiler_params=_FFN_PARAMS,
        name="ffn_first",
    )(x_prompt, x_sample, *weights)


def _ffn_final_call(x, n_prompt, layer, weights, final_norm):
    prompt_blocks = n_prompt // FFN_TOKEN_TILE
    n_tok = x.shape[0]
    tile = pl.BlockSpec((FFN_TOKEN_TILE, D_MODEL), lambda i: (i, 0))
    p_tile = pl.BlockSpec((FFN_TOKEN_TILE, D_MODEL), lambda i: (jnp.minimum(i, prompt_blocks - 1), 0))
    s_tile = pl.BlockSpec((FFN_TOKEN_TILE, D_MODEL), lambda i: (0, 0))
    return pl.pallas_call(
        functools.partial(_ffn_final_kernel, prompt_blocks=prompt_blocks),
        out_shape=(jax.ShapeDtypeStruct((n_prompt, D_MODEL), F32),
                   jax.ShapeDtypeStruct((n_tok - n_prompt, D_MODEL), F32)),
        grid=(n_tok // FFN_TOKEN_TILE,),
        in_specs=[tile] + _ffn_weight_specs(layer) + [_const_spec((1, D_MODEL), (0, 0))],
        out_specs=(p_tile, s_tile),
        compiler_params=_FFN_PARAMS,
        name="ffn_final",
    )(x, *weights, final_norm)


def _memkv_kernel(m_ref, g_ref, wk_ref, wv_ref, k_ref, v_ref, kb_ref, vb_ref):
    hn = _rmsnorm(m_ref[...], g_ref[...]).astype(BF16)
    k = _dot(hn, wk_ref[...])
    v = _dot(hn, wv_ref[...])
    n_mem = k.shape[0]
    for h in range(MEM_HEADS):
        hs = slice(h * MEM_HEAD_DIM, (h + 1) * MEM_HEAD_DIM)
        k_ref[pl.ds(h, n_mem, stride=MEM_HEADS), :] = k[:, hs]
        v_ref[pl.ds(h, n_mem, stride=MEM_HEADS), :] = v[:, hs]
    kb_ref[...] = k.astype(BF16)
    vb_ref[...] = v.astype(BF16)


def _memkv_call(mem, norm, wk, wv):
    depth = norm.shape[0]
    batch, n_mem, _ = mem.shape
    f32_spec = pl.BlockSpec((None, None, n_mem * MEM_HEADS, MEM_HEAD_DIM), lambda l, b: (l, b, 0, 0))
    out_spec = pl.BlockSpec((None, None, n_mem, MEM_DIM), lambda l, b: (l, b, 0, 0))
    w_spec = pl.BlockSpec((None, D_MODEL, MEM_DIM), lambda l, b: (l, 0, 0))
    f32_out = jax.ShapeDtypeStruct((depth, batch, n_mem * MEM_HEADS, MEM_HEAD_DIM), F32)
    bf16_out = jax.ShapeDtypeStruct((depth, batch, n_mem, MEM_DIM), BF16)
    return pl.pallas_call(
        _memkv_kernel,
        out_shape=(f32_out, f32_out, bf16_out, bf16_out),
        grid=(depth, batch),
        in_specs=[pl.BlockSpec((None, n_mem, D_MODEL), lambda l, b: (b, 0, 0)),
                  pl.BlockSpec((None, 1, D_MODEL), lambda l, b: (l, 0, 0)),
                  w_spec, w_spec],
        out_specs=(f32_spec, f32_spec, out_spec, out_spec),
        compiler_params=pltpu.CompilerParams(dimension_semantics=("arbitrary", "arbitrary")),
        name="mem_kv",
    )(mem, norm, wk, wv)


def _merge_out(x, z, graw, y_raw, y_pool, y_mem, gbias, snorm, wbp_ref, wbs_ref, wbm_ref, wo_ref):
    yz = y_raw * _silu(z)
    parts = []
    for g in range(SSM_GROUPS):
        yg = yz[:, g * SSM_GROUP_DIM:(g + 1) * SSM_GROUP_DIM]
        parts.append(yg * lax.rsqrt(jnp.mean(yg * yg, axis=-1, keepdims=True) + EPS))
    y_ssm = (jnp.concatenate(parts, axis=1) * snorm).astype(BF16)
    gates = jax.nn.sigmoid(graw + gbias)
    merged = (gates[:, :D_MODEL] * _dot(y_pool.astype(BF16), wbp_ref[...])
              + gates[:, D_MODEL:2 * D_MODEL] * _dot(y_ssm, wbs_ref[...])
              + gates[:, 2 * D_MODEL:] * _dot(y_mem.astype(BF16), wbm_ref[...]))
    return x + _dot(merged.astype(BF16), wo_ref[...])


def _attend(q, k, v):
    s = _dot_nt(q, k) * (MEM_HEAD_DIM ** -0.5)
    e = jnp.exp(s - jnp.max(s, axis=-1, keepdims=True))
    p = e * (1.0 / jnp.sum(e, axis=-1, keepdims=True))
    return _dot(p.astype(BF16), v)


def _prompt_mixer_kernel(x_ref, norm_ref, wpool_ref, wz_ref, wxbc_ref, wdt_ref, wq_ref, wg_ref,
                         gbias_ref, poolw_ref, pscale_ref, convw_ref, convb_ref, dtb_ref, alog_ref,
                         dskip_ref, snorm_ref, wbp_ref, wbs_ref, wbm_ref, wo_ref, k_ref, v_ref,
                         xo_ref, poolst_ref, convst_ref, ssmst_ref,
                         h_ref, cbuf_ref, pbuf_ref):
    T = PROMPT_CHUNK
    c = pl.program_id(1)
    last = pl.num_programs(1) - 1

    @pl.when(c == 0)
    def _():
        h_ref[...] = jnp.zeros_like(h_ref)
        cbuf_ref[0:SUBLANES, :] = jnp.zeros((SUBLANES, CONV_DIM), F32)
        pbuf_ref[0:2 * SUBLANES, :] = jnp.zeros((2 * SUBLANES, POOL_DIM), F32)

    x = x_ref[...]
    hn = _rmsnorm(x, norm_ref[...]).astype(BF16)

    P0 = 2 * SUBLANES
    u = _dot(hn, wpool_ref[...])
    pbuf_ref[P0:P0 + T, :] = u
    pos = c * T + lax.broadcasted_iota(jnp.int32, (T, 1), 0)
    y_parts = []
    for g, w in enumerate(POOL_WINDOWS):
        cs = slice(g * POOL_GROUP_DIM, (g + 1) * POOL_GROUP_DIM)
        ug = u[:, cs]
        s = ug
        for k in range(1, w):
            s = s + pbuf_ref[P0 - k:P0 - k + T, cs]
        inv_cnt = 1.0 / jnp.minimum(pos + 1, w).astype(F32)
        d = (s * inv_cnt - ug).astype(BF16)
        y_parts.append(_dot(d, poolw_ref[g]))
    y_pool = jnp.concatenate(y_parts, axis=1) * pscale_ref[...]

    @pl.when(c == last)
    def _():
        poolst_ref[...] = pbuf_ref[P0 + T - POOL_BUF:P0 + T, :]

    pbuf_ref[0:P0, :] = pbuf_ref[T:T + P0, :]

    C0 = SUBLANES
    xbc = _dot(hn, wxbc_ref[...])
    cbuf_ref[C0:C0 + T, :] = xbc
    acc = convb_ref[...] + convw_ref[CONV_WIDTH - 1:CONV_WIDTH, :] * xbc
    for k in range(CONV_WIDTH - 1):
        off = C0 - (CONV_WIDTH - 1) + k
        acc = acc + convw_ref[k:k + 1, :] * cbuf_ref[off:off + T, :]
    xc = _silu(acc)

    @pl.when(c == last)
    def _():
        convst_ref[...] = cbuf_ref[C0 + T - (CONV_WIDTH - 1):C0 + T, :]

    cbuf_ref[0:C0, :] = cbuf_ref[T:T + C0, :]

    xs = xc[:, :SSM_INNER]
    bm = xc[:, SSM_INNER:SSM_INNER + SSM_GROUPS * SSM_STATE]
    cm = xc[:, SSM_INNER + SSM_GROUPS * SSM_STATE:]

    dt = _softplus(_dot(hn, wdt_ref[...]) + dtb_ref[...])
    la = dt * (-jnp.exp(alog_ref[...]))
    la_t = la.T[0:SSM_HEADS, :]
    lane_t = lax.broadcasted_iota(jnp.int32, (SSM_HEADS, T), 1)
    acum_t = la_t
    shift = 1
    while shift < T:
        acum_t = acum_t + jnp.where(lane_t >= shift, pltpu.roll(acum_t, shift, 1), 0.0)
        shift *= 2
    acum = jnp.concatenate([acum_t, jnp.zeros((LANES - SSM_HEADS, T), F32)], axis=0).T
    from_start = jnp.exp(acum)
    to_end = jnp.exp(acum[T - 1:T, :] - acum)
    chunk_decay = jnp.exp(acum_t[:, T - 1:T])

    lane_g = lax.broadcasted_iota(jnp.int32, (T, SSM_GROUP_DIM), 1)
    causal = (lax.broadcasted_iota(jnp.int32, (T, T), 0) >= lax.broadcasted_iota(jnp.int32, (T, T), 1))
    y_groups = []
    for g in range(SSM_GROUPS):
        cm_g = cm[:, g * SSM_STATE:(g + 1) * SSM_STATE].astype(BF16)
        bm_g = bm[:, g * SSM_STATE:(g + 1) * SSM_STATE].astype(BF16)
        cb = _dot_nt(cm_g, bm_g)
        xdt = xs[:, g * SSM_GROUP_DIM:(g + 1) * SSM_GROUP_DIM] * _expand_group(dt, g, lane_g)
        lhs, rhs = [], []
        for j in range(SSM_HEADS_PER_GROUP):
            hh = SSM_HEADS_PER_GROUP * g + j
            seg = acum[:, hh:hh + 1] - acum_t[hh:hh + 1, :]
            decay = jnp.exp(jnp.where(causal, seg, -jnp.inf))
            lhs.append((cb * decay).astype(BF16))
            in_head = (lane_g >= SSM_HEAD_DIM * j) & (lane_g < SSM_HEAD_DIM * (j + 1))
            rhs.append(jnp.where(in_head, xdt, 0.0).astype(BF16))
        y_diag = _dot(jnp.concatenate(lhs, axis=1), jnp.concatenate(rhs, axis=0))
        rows = slice(g * SSM_GROUP_DIM, (g + 1) * SSM_GROUP_DIM)
        h_prev = h_ref[rows, :]
        y_off = _dot_nt(cm_g, h_prev.astype(BF16)) * _expand_group(from_start, g, lane_g)
        xw = (xdt * _expand_group(to_end, g, lane_g)).astype(BF16)
        st = lax.dot_general(xw, bm_g, _TN, preferred_element_type=F32)
        for j in range(SSM_HEADS_PER_GROUP):
            hh = SSM_HEADS_PER_GROUP * g + j
            r = slice(g * SSM_GROUP_DIM + j * SSM_HEAD_DIM, g * SSM_GROUP_DIM + (j + 1) * SSM_HEAD_DIM)
            h_ref[r, :] = (h_prev[j * SSM_HEAD_DIM:(j + 1) * SSM_HEAD_DIM, :] * chunk_decay[hh:hh + 1, :]
                           + st[j * SSM_HEAD_DIM:(j + 1) * SSM_HEAD_DIM, :])
        y_groups.append(y_diag + y_off)
    y_raw = jnp.concatenate(y_groups, axis=1) + dskip_ref[...] * xs

    @pl.when(c == last)
    def _():
        ssmst_ref[...] = h_ref[...]

    q = _dot(hn, wq_ref[...])
    o = []
    for h in range(MEM_HEADS):
        hs = slice(h * MEM_HEAD_DIM, (h + 1) * MEM_HEAD_DIM)
        o.append(_attend(q[:, hs].astype(BF16), k_ref[:, hs], v_ref[:, hs]))
    y_mem = jnp.concatenate(o, axis=1)

    z = _dot(hn, wz_ref[...])
    graw = _dot(hn, wg_ref[...])
    xo_ref[...] = _merge_out(x, z, graw, y_raw, y_pool, y_mem, gbias_ref[...], snorm_ref[...],
                             wbp_ref, wbs_ref, wbm_ref, wo_ref)


def _prompt_mixer_call(x, layer, batch, seq, p, kb, vb):
    T = PROMPT_CHUNK
    n_chunks = seq // T
    n_mem = kb.shape[2]
    x_spec = pl.BlockSpec((T, D_MODEL), lambda b, c: (b * n_chunks + c, 0))

    def wspec(rows, cols):
        return _const_spec((None, rows, cols), (layer, 0, 0))

    def vspec(cols):
        return _const_spec((None, 1, cols), (layer, 0, 0))

    kv_spec = pl.BlockSpec((None, None, n_mem, MEM_DIM), lambda b, c: (layer, b, 0, 0))
    in_specs = [x_spec, vspec(D_MODEL),
                wspec(D_MODEL, POOL_DIM), wspec(D_MODEL, SSM_INNER), wspec(D_MODEL, CONV_DIM),
                wspec(D_MODEL, LANES), wspec(D_MODEL, MEM_DIM), wspec(D_MODEL, N_BRANCH * D_MODEL),
                vspec(N_BRANCH * D_MODEL),
                _const_spec((None, 4, POOL_GROUP_DIM, POOL_GROUP_DIM), (layer, 0, 0, 0)),
                vspec(POOL_DIM), wspec(CONV_WIDTH, CONV_DIM), vspec(CONV_DIM), vspec(LANES), vspec(LANES),
                vspec(SSM_INNER), vspec(SSM_INNER),
                wspec(POOL_DIM, D_MODEL), wspec(SSM_INNER, D_MODEL), wspec(MEM_DIM, D_MODEL),
                wspec(D_MODEL, D_MODEL), kv_spec, kv_spec]
    out_shape = (jax.ShapeDtypeStruct(x.shape, F32),
                 jax.ShapeDtypeStruct((batch, POOL_BUF, POOL_DIM), F32),
                 jax.ShapeDtypeStruct((batch, CONV_WIDTH - 1, CONV_DIM), F32),
                 jax.ShapeDtypeStruct((batch, SSM_INNER, SSM_STATE), F32))
    out_specs = (x_spec,
                 pl.BlockSpec((None, POOL_BUF, POOL_DIM), lambda b, c: (b, 0, 0)),
                 pl.BlockSpec((None, CONV_WIDTH - 1, CONV_DIM), lambda b, c: (b, 0, 0)),
                 pl.BlockSpec((None, SSM_INNER, SSM_STATE), lambda b, c: (b, 0, 0)))
    return pl.pallas_call(
        _prompt_mixer_kernel,
        out_shape=out_shape,
        grid=(batch, n_chunks),
        in_specs=in_specs,
        out_specs=out_specs,
        scratch_shapes=[pltpu.VMEM((SSM_INNER, SSM_STATE), F32),
                        pltpu.VMEM((SUBLANES + T, CONV_DIM), F32),
                        pltpu.VMEM((2 * SUBLANES + T, POOL_DIM), F32)],
        input_output_aliases={0: 0},
        compiler_params=pltpu.CompilerParams(dimension_semantics=("arbitrary", "arbitrary"),
                                             vmem_limit_bytes=VMEM_LIMIT),
        name="prompt_mixer",
    )(x, p["mix_norm"], p["w_pool"], p["w_z"], p["w_xbc"], p["w_dt"], p["w_q"], p["w_g"],
      p["gate_bias"], p["pool_w"], p["pool_scale"], p["conv_w"], p["conv_b"], p["dt_bias"], p["a_log"],
      p["d_skip"], p["ssm_norm"], p["w_br_pool"], p["w_br_ssm"], p["w_br_mem"], p["w_o"], kb, vb)


def _sample_inproj_kernel(x_ref, norm_ref, wpool_ref, wz_ref, wxbc_ref, wdt_ref, wq_ref, wg_ref,
                          u_ref, z_ref, xbc_ref, dt_ref, q_ref, g_ref):
    hn = _rmsnorm(x_ref[...], norm_ref[...]).astype(BF16)
    u_ref[...] = _dot(hn, wpool_ref[...])
    z_ref[...] = _dot(hn, wz_ref[...])
    xbc_ref[...] = _dot(hn, wxbc_ref[...])
    dt_ref[...] = _dot(hn, wdt_ref[...])
    q_ref[...] = _dot(hn, wq_ref[...])
    g_ref[...] = _dot(hn, wg_ref[...])


def _sample_inproj_call(x, layer, row_block, n_rows, p):
    def wspec(cols):
        return _const_spec((None, D_MODEL, cols), (layer, 0, 0))

    widths = (POOL_DIM, SSM_INNER, CONV_DIM, LANES, MEM_DIM, N_BRANCH * D_MODEL)
    return pl.pallas_call(
        _sample_inproj_kernel,
        out_shape=tuple(jax.ShapeDtypeStruct((n_rows, w), F32) for w in widths),
        grid=(1,),
        in_specs=[pl.BlockSpec((n_rows, D_MODEL), lambda i: (row_block, 0)),
                  _const_spec((None, 1, D_MODEL), (layer, 0, 0)),
                  wspec(POOL_DIM), wspec(SSM_INNER), wspec(CONV_DIM), wspec(LANES), wspec(MEM_DIM),
                  wspec(N_BRANCH * D_MODEL)],
        out_specs=tuple(pl.BlockSpec((n_rows, w), lambda i: (0, 0)) for w in widths),
        compiler_params=pltpu.CompilerParams(dimension_semantics=("arbitrary",),
                                             vmem_limit_bytes=VMEM_LIMIT),
        name="sample_inproj",
    )(x, p["mix_norm"], p["w_pool"], p["w_z"], p["w_xbc"], p["w_dt"], p["w_q"], p["w_g"])


def _sample_ssd_kernel(cext_ref, pext_ref, dtext_ref, st_ref, convw_ref, convb_ref, dtb_ref, alog_ref,
                       dskip_ref, poolw_ref, pscale_ref,
                       y_ref, ypool_ref, stn_ref, yoff_ref):
    R = SAMPLE_SEQ_TILE * SEQ_ROWS
    n_tok = SEQ_ROWS // 2
    row = lax.broadcasted_iota(jnp.int32, (R, 1), 0)
    t_idx = row % SEQ_ROWS - n_tok
    tok = t_idx >= 0

    f = pext_ref[...]
    sums = []
    s = f
    for step in (1, 2, 4, 8):
        s = s + pltpu.roll(s, step, 0)
        sums.append(s)
    y_parts = []
    for g, w in enumerate(POOL_WINDOWS):
        cs = slice(g * POOL_GROUP_DIM, (g + 1) * POOL_GROUP_DIM)
        d = (sums[g][:, cs] * (1.0 / w) - f[:, cs]).astype(BF16)
        y_parts.append(_dot(d, poolw_ref[g]))
    ypool_ref[...] = jnp.concatenate(y_parts, axis=1) * pscale_ref[...]

    ext = cext_ref[...]
    acc = convb_ref[...] + convw_ref[CONV_WIDTH - 1:CONV_WIDTH, :] * ext
    for k in range(CONV_WIDTH - 1):
        acc = acc + convw_ref[k:k + 1, :] * pltpu.roll(ext, CONV_WIDTH - 1 - k, 0)
    xc = jnp.where(tok, _silu(acc), 0.0)
    xs = xc[:, :SSM_INNER]
    bm = xc[:, SSM_INNER:SSM_INNER + SSM_GROUPS * SSM_STATE]
    cm = xc[:, SSM_INNER + SSM_GROUPS * SSM_STATE:]

    dt = jnp.where(tok, _softplus(dtext_ref[...] + dtb_ref[...]), 0.0)
    la = dt * (-jnp.exp(alog_ref[...]))
    c1 = la + pltpu.roll(la, 1, 0)
    acum = c1 + pltpu.roll(c1, 2, 0)
    nxt = pltpu.roll(la, R - 1, 0)
    n1 = nxt + pltpu.roll(nxt, R - 1, 0)
    rest = n1 + pltpu.roll(n1, R - 2, 0)
    lane_g = lax.broadcasted_iota(jnp.int32, (R, SSM_GROUP_DIM), 1)
    lane_h = lax.broadcasted_iota(jnp.int32, (R, LANES), 1)

    xdt = xs * _expand_heads(dt, lane_g)
    xdt_q = xdt.astype(BF16).astype(F32)
    cm_q = cm.astype(BF16).astype(F32)
    bm_q = bm.astype(BF16).astype(F32)

    y_diag = jnp.zeros((R, SSM_INNER), F32)
    for k in range(n_tok):
        b_k = bm_q if k == 0 else pltpu.roll(bm_q, k, 0)
        prod = cm_q * b_k
        cb = jnp.zeros((R, LANES), F32)
        for g in reversed(range(SSM_GROUPS)):
            s_g = jnp.sum(prod[:, g * SSM_STATE:(g + 1) * SSM_STATE], axis=-1, keepdims=True)
            cb = jnp.where(lane_h < SSM_HEADS_PER_GROUP * (g + 1), s_g, cb)
        a_k = acum if k == 0 else pltpu.roll(acum, k, 0)
        decay = jnp.exp(jnp.where(t_idx >= k, acum - a_k, -jnp.inf))
        coef = (cb * decay).astype(BF16).astype(F32)
        x_k = xdt_q if k == 0 else pltpu.roll(xdt_q, k, 0)
        y_diag = y_diag + _expand_heads(coef, lane_g) * x_k

    xw_t = (xdt * _expand_heads(jnp.exp(rest), lane_g)).T
    total_t = jnp.exp(acum + rest).T
    bm_b = bm.astype(BF16)
    col = lax.broadcasted_iota(jnp.int32, (SSM_GROUP_DIM, R), 1)
    for b in range(SAMPLE_SEQ_TILE):
        r0 = b * SEQ_ROWS
        c_rows = cm[r0:r0 + SEQ_ROWS, :]
        in_seq = (col >= r0) & (col < r0 + SEQ_ROWS)
        last_col = r0 + SEQ_ROWS - 1
        for g in range(SSM_GROUPS):
            rows = slice(g * SSM_GROUP_DIM, (g + 1) * SSM_GROUP_DIM)
            h0 = st_ref[b, rows, :]
            c_g = c_rows[:, g * SSM_STATE:(g + 1) * SSM_STATE].astype(BF16)
            yoff_ref[r0:r0 + SEQ_ROWS, rows] = _dot_nt(c_g, h0.astype(BF16))
            lhs = jnp.where(in_seq, xw_t[rows, :], 0.0).astype(BF16)
            st = _dot(lhs, bm_b[:, g * SSM_STATE:(g + 1) * SSM_STATE])
            for j in range(SSM_HEADS_PER_GROUP):
                hh = SSM_HEADS_PER_GROUP * g + j
                hr = slice(j * SSM_HEAD_DIM, (j + 1) * SSM_HEAD_DIM)
                stn_ref[b, g * SSM_GROUP_DIM + j * SSM_HEAD_DIM:g * SSM_GROUP_DIM + (j + 1) * SSM_HEAD_DIM, :] = (
                    h0[hr, :] * total_t[hh:hh + 1, last_col:last_col + 1] + st[hr, :])

    y_ref[...] = (y_diag + yoff_ref[...] * _expand_heads(jnp.exp(acum), lane_g) + dskip_ref[...] * xs)


def _sample_ssd_chained_kernel(cext_ref, pext_ref, dtext_ref, st_ref, convw_ref, convb_ref, dtb_ref, alog_ref,
                              dskip_ref, poolw_ref, pscale_ref, prev_ref, *rest):
    del prev_ref
    _sample_ssd_kernel(cext_ref, pext_ref, dtext_ref, st_ref, convw_ref, convb_ref, dtb_ref, alog_ref,
                       dskip_ref, poolw_ref, pscale_ref, *rest)


def _sample_ssd_call(layer, cext, pext, dtext, state, new_state, p):
    n_seq = state.shape[1]
    R = SAMPLE_SEQ_TILE * SEQ_ROWS
    RP = SAMPLE_SEQ_TILE * POOL_SEQ_ROWS

    def vspec(rows, cols):
        return _const_spec((None, rows, cols), (layer, 0, 0))

    st_spec = pl.BlockSpec((None, SAMPLE_SEQ_TILE, SSM_INNER, SSM_STATE), lambda i: (layer, i, 0, 0))
    in_specs = [pl.BlockSpec((R, CONV_DIM), lambda i: (i, 0)),
                pl.BlockSpec((RP, POOL_DIM), lambda i: (i, 0)),
                pl.BlockSpec((R, LANES), lambda i: (i, 0)),
                st_spec,
                vspec(CONV_WIDTH, CONV_DIM), vspec(1, CONV_DIM), vspec(1, LANES), vspec(1, LANES),
                vspec(1, SSM_INNER),
                _const_spec((None, 4, POOL_GROUP_DIM, POOL_GROUP_DIM), (layer, 0, 0, 0)),
                vspec(1, POOL_DIM)]
    args = [cext, pext, dtext, state, p["conv_w"], p["conv_b"], p["dt_bias"], p["a_log"], p["d_skip"],
            p["pool_w"], p["pool_scale"]]
    kern, aliases = _sample_ssd_kernel, {}
    if new_state is not None:
        in_specs.append(pl.BlockSpec(memory_space=pl.ANY))
        args.append(new_state)
        kern, aliases = _sample_ssd_chained_kernel, {len(args) - 1: 2}
    return pl.pallas_call(
        kern,
        out_shape=(jax.ShapeDtypeStruct((n_seq * SEQ_ROWS, SSM_INNER), F32),
                   jax.ShapeDtypeStruct((n_seq * POOL_SEQ_ROWS, POOL_DIM), F32),
                   jax.ShapeDtypeStruct(state.shape, F32)),
        grid=(n_seq // SAMPLE_SEQ_TILE,),
        in_specs=in_specs,
        out_specs=(pl.BlockSpec((R, SSM_INNER), lambda i: (i, 0)),
                   pl.BlockSpec((RP, POOL_DIM), lambda i: (i, 0)),
                   st_spec),
        scratch_shapes=[pltpu.VMEM((R, SSM_INNER), F32)],
        input_output_aliases=aliases,
        compiler_params=pltpu.CompilerParams(dimension_semantics=("arbitrary",),
                                             vmem_limit_bytes=VMEM_LIMIT),
        name="sample_ssd",
    )(*args)


def _sample_attn_kernel(q_ref, k_ref, v_ref, o_ref, s_ref):
    n_mem = k_ref.shape[1] // MEM_HEADS
    HR = MEM_HEADS * SEQ_ROWS
    row_head = lax.broadcasted_iota(jnp.int32, (HR, MEM_DIM), 0) // SEQ_ROWS
    col_head = lax.broadcasted_iota(jnp.int32, (HR, MEM_DIM), 1) // MEM_HEAD_DIM
    own_head = row_head == col_head

    def heads_on_lanes(ref, b):
        return jnp.concatenate([ref[b, pl.ds(h, n_mem, stride=MEM_HEADS), :] for h in range(MEM_HEADS)],
                               axis=1).astype(BF16)

    for b in range(SAMPLE_ATTN_TILE):
        q = q_ref[b * SEQ_ROWS:(b + 1) * SEQ_ROWS, :]
        q_heads = jnp.where(own_head, jnp.concatenate([q] * MEM_HEADS, axis=0), 0.0).astype(BF16)
        s_ref[b * HR:(b + 1) * HR, :] = _dot_nt(q_heads, heads_on_lanes(k_ref, b))
    s = s_ref[...] * (MEM_HEAD_DIM ** -0.5)
    e = jnp.exp(s - jnp.max(s, axis=-1, keepdims=True))
    p = (e * (1.0 / jnp.sum(e, axis=-1, keepdims=True))).astype(BF16)
    for b in range(SAMPLE_ATTN_TILE):
        o = _dot(p[b * HR:(b + 1) * HR, :], heads_on_lanes(v_ref, b))
        o_ref[b * SEQ_ROWS:(b + 1) * SEQ_ROWS, :] = jnp.concatenate(
            [o[h * SEQ_ROWS:(h + 1) * SEQ_ROWS, h * MEM_HEAD_DIM:(h + 1) * MEM_HEAD_DIM]
             for h in range(MEM_HEADS)], axis=1)


def _sample_attn_call(layer, qext, cache_k, cache_v):
    n_seq, kv_rows = cache_k.shape[1], cache_k.shape[2]
    R = SAMPLE_ATTN_TILE * SEQ_ROWS
    kv_spec = pl.BlockSpec((None, SAMPLE_ATTN_TILE, kv_rows, MEM_HEAD_DIM), lambda i: (layer, i, 0, 0))
    return pl.pallas_call(
        _sample_attn_kernel,
        out_shape=jax.ShapeDtypeStruct((n_seq * SEQ_ROWS, MEM_DIM), F32),
        grid=(n_seq // SAMPLE_ATTN_TILE,),
        in_specs=[pl.BlockSpec((R, MEM_DIM), lambda i: (i, 0)), kv_spec, kv_spec],
        out_specs=pl.BlockSpec((R, MEM_DIM), lambda i: (i, 0)),
        scratch_shapes=[pltpu.VMEM((SAMPLE_ATTN_TILE * MEM_HEADS * SEQ_ROWS, kv_rows // MEM_HEADS), F32)],
        compiler_params=pltpu.CompilerParams(dimension_semantics=("arbitrary",),
                                             vmem_limit_bytes=VMEM_LIMIT),
        name="sample_attn",
    )(qext, cache_k, cache_v)


def _sample_merge_kernel(x_ref, z_ref, g_ref, y_ref, ypool_ref, ymem_ref, gbias_ref, snorm_ref,
                         wbp_ref, wbs_ref, wbm_ref, wo_ref, xo_ref):
    xo_ref[...] = _merge_out(x_ref[...], z_ref[...], g_ref[...], y_ref[...], ypool_ref[...], ymem_ref[...],
                             gbias_ref[...], snorm_ref[...], wbp_ref, wbs_ref, wbm_ref, wo_ref)


def _sample_merge_call(x, layer, row_block, n_rows, z, graw, y_raw, y_pool, y_mem, p):
    def full(cols):
        return pl.BlockSpec((n_rows, cols), lambda i: (0, 0))

    def wspec(rows, cols):
        return _const_spec((None, rows, cols), (layer, 0, 0))

    x_spec = pl.BlockSpec((n_rows, D_MODEL), lambda i: (row_block, 0))
    return pl.pallas_call(
        _sample_merge_kernel,
        out_shape=jax.ShapeDtypeStruct(x.shape, F32),
        grid=(1,),
        in_specs=[x_spec, full(SSM_INNER), full(N_BRANCH * D_MODEL), full(SSM_INNER), full(POOL_DIM),
                  full(MEM_DIM), wspec(1, N_BRANCH * D_MODEL), wspec(1, SSM_INNER),
                  wspec(POOL_DIM, D_MODEL), wspec(SSM_INNER, D_MODEL), wspec(MEM_DIM, D_MODEL),
                  wspec(D_MODEL, D_MODEL)],
        out_specs=x_spec,
        input_output_aliases={0: 0},
        compiler_params=pltpu.CompilerParams(dimension_semantics=("arbitrary",),
                                             vmem_limit_bytes=VMEM_LIMIT),
        name="sample_merge",
    )(x, z, graw, y_raw, y_pool, y_mem, p["gate_bias"], p["ssm_norm"],
      p["w_br_pool"], p["w_br_ssm"], p["w_br_mem"], p["w_o"])


def _row3(a):
    return a.reshape(a.shape[0], 1, a.shape[1])


def _pad_lanes(a):
    return jnp.pad(a, [(0, 0)] * (a.ndim - 1) + [(0, LANES - a.shape[-1])])


def _token_rows(ext, rows_per_seq, n_tok):
    n_seq = ext.shape[0] // rows_per_seq
    return ext.reshape(n_seq, rows_per_seq, -1)[:, rows_per_seq - n_tok:, :].reshape(n_seq * n_tok, -1)


def kernel(x_prompt, x_sample, mem_prompt, state_pool, state_conv, state_ssm, cache_mem_k, cache_mem_v,
           ffn1_norm, ffn1_w_gate, ffn1_w_up, ffn1_w_down, mix_norm, w_in, gate_bias,
           pool_w, pool_scale, conv_w, conv_b, dt_bias, a_log, d_skip, ssm_norm,
           mem_norm, w_mem_k, w_mem_v, w_br_pool, w_br_ssm, w_br_mem, w_o,
           ffn2_norm, ffn2_w_gate, ffn2_w_up, ffn2_w_down, final_norm):
    depth = w_in.shape[0]
    batch, seq, _ = x_prompt.shape
    n_seq, n_tok, _ = x_sample.shape
    n_mem = mem_prompt.shape[1]
    n_prompt = batch * seq
    n_sample = n_seq * n_tok
    assert n_sample == FFN_TOKEN_TILE and n_prompt % FFN_TOKEN_TILE == 0 and seq % PROMPT_CHUNK == 0
    assert n_tok == SEQ_ROWS // 2 and PAST_LEN >= max(POOL_WINDOWS)
    assert n_seq % SAMPLE_SEQ_TILE == 0 and n_seq % SAMPLE_ATTN_TILE == 0
    sample_block = n_prompt // n_sample

    cuts = [0]
    for w in IN_SPLITS:
        cuts.append(cuts[-1] + w)

    def w_in_cols(i):
        return w_in[:, :, cuts[i]:cuts[i + 1]].astype(BF16)

    params = dict(
        mix_norm=_row3(mix_norm),
        w_pool=w_in_cols(0), w_z=w_in_cols(1), w_xbc=w_in_cols(2), w_dt=_pad_lanes(w_in_cols(3)),
        w_q=w_in_cols(4), w_g=w_in_cols(5),
        gate_bias=_row3(gate_bias), pool_w=pool_w.astype(BF16), pool_scale=_row3(pool_scale),
        conv_w=conv_w, conv_b=_row3(conv_b), dt_bias=_row3(_pad_lanes(dt_bias)), a_log=_row3(_pad_lanes(a_log)),
        d_skip=_row3(jnp.repeat(d_skip, SSM_HEAD_DIM, axis=1)), ssm_norm=_row3(ssm_norm),
        w_br_pool=w_br_pool.astype(BF16), w_br_ssm=w_br_ssm.astype(BF16), w_br_mem=w_br_mem.astype(BF16),
        w_o=w_o.astype(BF16))
    ffn1 = (_row3(ffn1_norm), ffn1_w_gate.astype(BF16), ffn1_w_up.astype(BF16), ffn1_w_down.astype(BF16))
    ffn2 = (_row3(ffn2_norm), ffn2_w_gate.astype(BF16), ffn2_w_up.astype(BF16), ffn2_w_down.astype(BF16))

    mem_k, mem_v, mem_kb, mem_vb = _memkv_call(mem_prompt, _row3(mem_norm), w_mem_k.astype(BF16),
                                               w_mem_v.astype(BF16))
    cache_k = cache_mem_k.reshape(depth, n_seq, n_mem * MEM_HEADS, MEM_HEAD_DIM)
    cache_v = cache_mem_v.reshape(depth, n_seq, n_mem * MEM_HEADS, MEM_HEAD_DIM)
    state_ssm_rows = state_ssm.reshape(depth, n_seq, SSM_INNER, SSM_STATE)

    pool_p, conv_p, ssm_p, pool_s, conv_s = [], [], [], [], []
    ssm_s = None
    for l in range(depth):
        if l == 0:
            x = _ffn_first_call(x_prompt.reshape(n_prompt, D_MODEL), x_sample.reshape(n_sample, D_MODEL), l, ffn1)
        else:
            x = _ffn_call(x, l, ffn1)
        x, npool, nconv, nssm = _prompt_mixer_call(x, l, batch, seq, params, mem_kb, mem_vb)
        pool_p.append(npool)
        conv_p.append(nconv)
        ssm_p.append(nssm.reshape(batch, SSM_HEADS, SSM_HEAD_DIM, SSM_STATE))

        u, z, xbc, dt_raw, q, graw = _sample_inproj_call(x, l, sample_block, n_sample, params)
        u3 = u.reshape(n_seq, n_tok, POOL_DIM)
        xbc3 = xbc.reshape(n_seq, n_tok, CONV_DIM)
        pext = jnp.concatenate([jnp.zeros((n_seq, POOL_SEQ_ROWS - POOL_BUF - n_tok, POOL_DIM), F32),
                                state_pool[l], u3], axis=1).reshape(n_seq * POOL_SEQ_ROWS, POOL_DIM)
        cext = jnp.concatenate([jnp.zeros((n_seq, SEQ_ROWS - CONV_WIDTH + 1 - n_tok, CONV_DIM), F32),
                                state_conv[l], xbc3], axis=1).reshape(n_seq * SEQ_ROWS, CONV_DIM)

        def pad_seq(a):
            a3 = a.reshape(n_seq, n_tok, a.shape[-1])
            return jnp.concatenate([jnp.zeros_like(a3), a3], axis=1).reshape(n_seq * SEQ_ROWS, a.shape[-1])

        y_ext, ypool_ext, ssm_s = _sample_ssd_call(l, cext, pext, pad_seq(dt_raw), state_ssm_rows, ssm_s, params)
        ymem_ext = _sample_attn_call(l, pad_seq(q), cache_k, cache_v)
        x = _sample_merge_call(x, l, sample_block, n_sample, z, graw,
                               _token_rows(y_ext, SEQ_ROWS, n_tok), _token_rows(ypool_ext, POOL_SEQ_ROWS, n_tok),
                               _token_rows(ymem_ext, SEQ_ROWS, n_tok), params)
        pool_s.append(jnp.concatenate([state_pool[l][:, n_tok:], u3], axis=1))
        conv_s.append(xbc3[:, n_tok - (CONV_WIDTH - 1):])

        if l < depth - 1:
            x = _ffn_call(x, l, ffn2)
    y_prompt, y_sample = _ffn_final_call(x, n_prompt, depth - 1, ffn2, final_norm.reshape(1, D_MODEL))

    mem_shape = (depth, batch, n_mem, MEM_HEADS, MEM_HEAD_DIM)
    return (y_prompt.reshape(batch, seq, D_MODEL), y_sample.reshape(n_seq, n_tok, D_MODEL),
            jnp.stack(pool_p), jnp.stack(conv_p), jnp.stack(ssm_p),
            mem_k.reshape(mem_shape), mem_v.reshape(mem_shape),
            jnp.stack(pool_s), jnp.stack(conv_s),
            ssm_s.reshape(depth, n_seq, SSM_HEADS, SSM_HEAD_DIM, SSM_STATE))
```

```python
import functools

import jax
import jax.numpy as jnp
from jax import lax
from jax.experimental import pallas as pl
from jax.experimental.pallas import tpu as pltpu

F32 = jnp.float32
BF16 = jnp.bfloat16

D_MODEL = 1024
D_FF = 2816
EPS = 1e-6
PAST_LEN = 16384

POOL_WINDOWS = (2, 4, 8, 16)
POOL_GROUP_DIM = 128
POOL_DIM = 512
POOL_BUF = 15
SSM_INNER = 1024
SSM_HEADS = 16
SSM_HEAD_DIM = 64
SSM_GROUPS = 4
SSM_HEADS_PER_GROUP = 4
SSM_GROUP_DIM = 256
SSM_STATE = 128
CONV_WIDTH = 4
CONV_DIM = 2048
MEM_HEADS = 4
MEM_HEAD_DIM = 128
MEM_DIM = 512
N_BRANCH = 3
IN_SPLITS = (POOL_DIM, SSM_INNER, CONV_DIM, SSM_HEADS, MEM_DIM, N_BRANCH * D_MODEL)

SUBLANES = 8
LANES = 128

FFN_TOKEN_TILE = 512
FFN_FF_CHUNK = 2816
PROMPT_CHUNK = 256
SAMPLE_SEQ_TILE = 16
SAMPLE_ATTN_TILE = 8
MEMKV_BATCH_TILE = 4
SEQ_ROWS = 8
POOL_SEQ_ROWS = 24
VMEM_LIMIT = 56 * 1024 * 1024

_NT = (((1,), (1,)), ((), ()))
_TN = (((0,), (0,)), ((), ()))


def _dot(a, b):
    return jnp.dot(a, b, preferred_element_type=F32)


def _dot_nt(a, b):
    return lax.dot_general(a, b, _NT, preferred_element_type=F32)


def _rmsnorm(x, g):
    return x * lax.rsqrt(jnp.mean(x * x, axis=-1, keepdims=True) + EPS) * g


def _silu(x):
    return x * jax.nn.sigmoid(x)


def _softplus(x):
    return jnp.maximum(x, 0.0) + jnp.log1p(jnp.exp(-jnp.abs(x)))


def _const_spec(shape, index, single_buffer=True):
    mode = pl.Buffered(1) if single_buffer else None
    return pl.BlockSpec(shape, lambda *_: index, pipeline_mode=mode)


def _expand_group(v, g, lane):
    h0 = SSM_HEADS_PER_GROUP * g
    out = jnp.broadcast_to(v[:, h0 + 3:h0 + 4], lane.shape)
    for j in (2, 1, 0):
        out = jnp.where(lane < SSM_HEAD_DIM * (j + 1), v[:, h0 + j:h0 + j + 1], out)
    return out


def _expand_heads(v, lane):
    return jnp.concatenate([_expand_group(v, g, lane) for g in range(SSM_GROUPS)], axis=1)


def _ffn_body(x, g, wg_ref, wu_ref, wd_ref):
    hn = _rmsnorm(x, g).astype(BF16)
    acc = None
    for j in range(D_FF // FFN_FF_CHUNK):
        sl = slice(j * FFN_FF_CHUNK, (j + 1) * FFN_FF_CHUNK)
        gate = _dot(hn, wg_ref[:, sl])
        up = _dot(hn, wu_ref[:, sl])
        part = _dot((_silu(gate) * up).astype(BF16), wd_ref[sl, :])
        acc = part if acc is None else acc + part
    return x + 0.5 * acc


def _ffn_kernel(x_ref, g_ref, wg_ref, wu_ref, wd_ref, o_ref):
    o_ref[...] = _ffn_body(x_ref[...], g_ref[...], wg_ref, wu_ref, wd_ref)


def _ffn_first_kernel(xp_ref, xs_ref, g_ref, wg_ref, wu_ref, wd_ref, o_ref, *, prompt_blocks):
    x = jnp.where(pl.program_id(0) < prompt_blocks, xp_ref[...], xs_ref[...])
    o_ref[...] = _ffn_body(x, g_ref[...], wg_ref, wu_ref, wd_ref)


def _ffn_final_kernel(x_ref, g_ref, wg_ref, wu_ref, wd_ref, fg_ref, yp_ref, ys_ref, *, prompt_blocks):
    y = _rmsnorm(_ffn_body(x_ref[...], g_ref[...], wg_ref, wu_ref, wd_ref), fg_ref[...])
    i = pl.program_id(0)

    @pl.when(i < prompt_blocks)
    def _():
        yp_ref[...] = y

    @pl.when(i == prompt_blocks)
    def _():
        ys_ref[...] = y


def _ffn_weight_specs(layer):
    return [_const_spec((None, 1, D_MODEL), (layer, 0, 0)),
            _const_spec((None, D_MODEL, D_FF), (layer, 0, 0)),
            _const_spec((None, D_MODEL, D_FF), (layer, 0, 0)),
            _const_spec((None, D_FF, D_MODEL), (layer, 0, 0))]


_FFN_PARAMS = pltpu.CompilerParams(dimension_semantics=("arbitrary",), vmem_limit_bytes=VMEM_LIMIT)


def _ffn_call(x, layer, weights):
    tile = pl.BlockSpec((FFN_TOKEN_TILE, D_MODEL), lambda i: (i, 0))
    return pl.pallas_call(
        _ffn_kernel,
        out_shape=jax.ShapeDtypeStruct(x.shape, F32),
        grid=(x.shape[0] // FFN_TOKEN_TILE,),
        in_specs=[tile] + _ffn_weight_specs(layer),
        out_specs=tile,
        input_output_aliases={0: 0},
        compiler_params=_FFN_PARAMS,
        name="ffn",
    )(x, *weights)


def _ffn_first_call(x_prompt, x_sample, layer, weights):
    prompt_blocks = x_prompt.shape[0] // FFN_TOKEN_TILE
    n_tok = x_prompt.shape[0] + x_sample.shape[0]
    tile = pl.BlockSpec((FFN_TOKEN_TILE, D_MODEL), lambda i: (i, 0))
    p_tile = pl.BlockSpec((FFN_TOKEN_TILE, D_MODEL), lambda i: (jnp.minimum(i, prompt_blocks - 1), 0))
    s_tile = pl.BlockSpec((FFN_TOKEN_TILE, D_MODEL), lambda i: (0, 0))
    return pl.pallas_call(
        functools.partial(_ffn_first_kernel, prompt_blocks=prompt_blocks),
        out_shape=jax.ShapeDtypeStruct((n_tok, D_MODEL), F32),
        grid=(n_tok // FFN_TOKEN_TILE,),
        in_specs=[p_tile, s_tile] + _ffn_weight_specs(layer),
        out_specs=tile,
        compiler_params=_FFN_PARAMS,
        name="ffn_first",
    )(x_prompt, x_sample, *weights)


def _ffn_final_call(x, n_prompt, layer, weights, final_norm):
    prompt_blocks = n_prompt // FFN_TOKEN_TILE
    n_tok = x.shape[0]
    tile = pl.BlockSpec((FFN_TOKEN_TILE, D_MODEL), lambda i: (i, 0))
    p_tile = pl.BlockSpec((FFN_TOKEN_TILE, D_MODEL), lambda i: (jnp.minimum(i, prompt_blocks - 1), 0))
    s_tile = pl.BlockSpec((FFN_TOKEN_TILE, D_MODEL), lambda i: (0, 0))
    return pl.pallas_call(
        functools.partial(_ffn_final_kernel, prompt_blocks=prompt_blocks),
        out_shape=(jax.ShapeDtypeStruct((n_prompt, D_MODEL), F32),
                   jax.ShapeDtypeStruct((n_tok - n_prompt, D_MODEL), F32)),
        grid=(n_tok // FFN_TOKEN_TILE,),
        in_specs=[tile] + _ffn_weight_specs(layer) + [_const_spec((1, D_MODEL), (0, 0))],
        out_specs=(p_tile, s_tile),
        compiler_params=_FFN_PARAMS,
        name="ffn_final",
    )(x, *weights, final_norm)


def _memkv_kernel(m_ref, g_ref, wk_ref, wv_ref, k_ref, v_ref, kb_ref, vb_ref):
    nb, n_mem, _ = m_ref.shape
    hn = _rmsnorm(m_ref[...].reshape(nb * n_mem, D_MODEL), g_ref[...]).astype(BF16)
    k = _dot(hn, wk_ref[...])
    v = _dot(hn, wv_ref[...])
    for b in range(nb):
        rows = slice(b * n_mem, (b + 1) * n_mem)
        for h in range(MEM_HEADS):
            hs = slice(h * MEM_HEAD_DIM, (h + 1) * MEM_HEAD_DIM)
            k_ref[b, pl.ds(h, n_mem, stride=MEM_HEADS), :] = k[rows, hs]
            v_ref[b, pl.ds(h, n_mem, stride=MEM_HEADS), :] = v[rows, hs]
        kb_ref[b] = k[rows, :].astype(BF16)
        vb_ref[b] = v[rows, :].astype(BF16)


def _memkv_call(mem, norm, wk, wv):
    depth = norm.shape[0]
    batch, n_mem, _ = mem.shape
    nb = MEMKV_BATCH_TILE
    f32_spec = pl.BlockSpec((None, nb, n_mem * MEM_HEADS, MEM_HEAD_DIM), lambda l, b: (l, b, 0, 0))
    out_spec = pl.BlockSpec((None, nb, n_mem, MEM_DIM), lambda l, b: (l, b, 0, 0))
    w_spec = pl.BlockSpec((None, D_MODEL, MEM_DIM), lambda l, b: (l, 0, 0))
    f32_out = jax.ShapeDtypeStruct((depth, batch, n_mem * MEM_HEADS, MEM_HEAD_DIM), F32)
    bf16_out = jax.ShapeDtypeStruct((depth, batch, n_mem, MEM_DIM), BF16)
    return pl.pallas_call(
        _memkv_kernel,
        out_shape=(f32_out, f32_out, bf16_out, bf16_out),
        grid=(depth, batch // nb),
        in_specs=[pl.BlockSpec((nb, n_mem, D_MODEL), lambda l, b: (b, 0, 0)),
                  pl.BlockSpec((None, 1, D_MODEL), lambda l, b: (l, 0, 0)),
                  w_spec, w_spec],
        out_specs=(f32_spec, f32_spec, out_spec, out_spec),
        compiler_params=pltpu.CompilerParams(dimension_semantics=("arbitrary", "arbitrary"),
                                             vmem_limit_bytes=VMEM_LIMIT),
        name="mem_kv",
    )(mem, norm, wk, wv)


def _merge_tail(x, silu_z, gates, y_raw, y_pool, y_mem, snorm, wbp_ref, wbs_ref, wbm_ref, wo_ref):
    yz = y_raw * silu_z
    parts = []
    for g in range(SSM_GROUPS):
        yg = yz[:, g * SSM_GROUP_DIM:(g + 1) * SSM_GROUP_DIM]
        parts.append(yg * lax.rsqrt(jnp.mean(yg * yg, axis=-1, keepdims=True) + EPS))
    y_ssm = (jnp.concatenate(parts, axis=1) * snorm).astype(BF16)
    merged = (gates[:, :D_MODEL] * _dot(y_pool.astype(BF16), wbp_ref[...])
              + gates[:, D_MODEL:2 * D_MODEL] * _dot(y_ssm, wbs_ref[...])
              + gates[:, 2 * D_MODEL:] * _dot(y_mem.astype(BF16), wbm_ref[...]))
    return x + _dot(merged.astype(BF16), wo_ref[...])


def _attend(q, k, v):
    s = _dot_nt(q, k) * (MEM_HEAD_DIM ** -0.5)
    e = jnp.exp(s - jnp.max(s, axis=-1, keepdims=True))
    p = e * (1.0 / jnp.sum(e, axis=-1, keepdims=True))
    return _dot(p.astype(BF16), v)


def _prompt_mixer_kernel(x_ref, norm_ref, wpool_ref, wz_ref, wxbc_ref, wdt_ref, wq_ref, wg_ref,
                         gbias_ref, poolw_ref, pscale_ref, convw_ref, convb_ref, dtb_ref, alog_ref,
                         dskip_ref, snorm_ref, wbp_ref, wbs_ref, wbm_ref, wo_ref, k_ref, v_ref,
                         xo_ref, poolst_ref, convst_ref, h_ref,
                         cbuf_ref, pbuf_ref):
    T = PROMPT_CHUNK
    W = SSM_GROUP_DIM
    c = pl.program_id(1)

    @pl.when(c == 0)
    def _():
        h_ref[...] = jnp.zeros_like(h_ref)
        cbuf_ref[0:SUBLANES, :] = jnp.zeros((SUBLANES, CONV_DIM), F32)
        pbuf_ref[0:2 * SUBLANES, :] = jnp.zeros((2 * SUBLANES, POOL_DIM), F32)

    x = x_ref[...]
    hn = _rmsnorm(x, norm_ref[...]).astype(BF16)

    dt = _softplus(_dot(hn, wdt_ref[...]) + dtb_ref[...])
    la = dt * (-jnp.exp(alog_ref[...]))
    la_t = la.T[0:SSM_HEADS, :]
    lane_t = lax.broadcasted_iota(jnp.int32, (SSM_HEADS, T), 1)
    acum_t = la_t
    shift = 1
    while shift < T:
        acum_t = acum_t + jnp.where(lane_t >= shift, pltpu.roll(acum_t, shift, 1), 0.0)
        shift *= 2
    acum = jnp.concatenate([acum_t, jnp.zeros((LANES - SSM_HEADS, T), F32)], axis=0).T
    from_start = jnp.exp(acum)
    to_end = jnp.exp(acum[T - 1:T, :] - acum)
    chunk_decay = jnp.exp(acum_t[:, T - 1:T])

    P0 = 2 * SUBLANES
    u = _dot(hn, wpool_ref[...])
    pbuf_ref[P0:P0 + T, :] = u
    pos = c * T + lax.broadcasted_iota(jnp.int32, (T, 1), 0)

    C0 = SUBLANES
    n_blocks = CONV_DIM // W
    n_x = SSM_INNER // W
    xc = [None] * n_blocks
    wide_out = [None] * n_blocks
    y_pool = [None] * len(POOL_WINDOWS)
    q = _dot(hn, wq_ref[...])
    lane_g = lax.broadcasted_iota(jnp.int32, (T, W), 1)
    causal = (lax.broadcasted_iota(jnp.int32, (T, T), 0) >= lax.broadcasted_iota(jnp.int32, (T, T), 1))

    def wide_block(i):
        lo, hi = 2 * i * W, 2 * (i + 1) * W
        if hi <= N_BRANCH * D_MODEL:
            wide_out[i] = jax.nn.sigmoid(_dot(hn, wg_ref[:, lo:hi]) + gbias_ref[:, lo:hi])
        else:
            lo, hi = lo - N_BRANCH * D_MODEL, hi - N_BRANCH * D_MODEL
            wide_out[i] = _silu(_dot(hn, wz_ref[:, lo:hi]))

    def conv_block(j):
        cols = slice(j * W, (j + 1) * W)
        xbc = _dot(hn, wxbc_ref[:, cols])
        cbuf_ref[C0:C0 + T, cols] = xbc
        acc = convb_ref[:, cols] + convw_ref[CONV_WIDTH - 1:CONV_WIDTH, cols] * xbc
        for k in range(CONV_WIDTH - 1):
            off = C0 - (CONV_WIDTH - 1) + k
            acc = acc + convw_ref[k:k + 1, cols] * cbuf_ref[off:off + T, cols]
        xc[j] = _silu(acc)
        convst_ref[:, cols] = xbc[T - (CONV_WIDTH - 1):, :]
        cbuf_ref[0:C0, cols] = xbc[T - C0:, :]

    def pool_group(j):
        w = POOL_WINDOWS[j]
        cs = slice(j * POOL_GROUP_DIM, (j + 1) * POOL_GROUP_DIM)
        ug = u[:, cs]
        s = ug
        for k in range(1, w):
            s = s + pbuf_ref[P0 - k:P0 - k + T, cs]
        inv_cnt = 1.0 / jnp.minimum(pos + 1, w).astype(F32)
        y_pool[j] = _dot((s * inv_cnt - ug).astype(BF16), poolw_ref[j])

    def ssd_group(g):
        half = slice((g % 2) * SSM_STATE, (g % 2 + 1) * SSM_STATE)
        xs_g = xc[g]
        bm_g = xc[n_x + g // 2][:, half].astype(BF16)
        cm_g = xc[n_x + SSM_GROUPS // 2 + g // 2][:, half].astype(BF16)
        cb = _dot_nt(cm_g, bm_g)
        xdt = xs_g * _expand_group(dt, g, lane_g)
        xdt_b = xdt.astype(BF16)
        lhs, rhs = [], []
        for j in range(SSM_HEADS_PER_GROUP):
            hh = SSM_HEADS_PER_GROUP * g + j
            seg = acum[:, hh:hh + 1] - acum_t[hh:hh + 1, :]
            decay = jnp.exp(jnp.where(causal, seg, -jnp.inf))
            lhs.append((cb * decay).astype(BF16))
            in_head = (lane_g >= SSM_HEAD_DIM * j) & (lane_g < SSM_HEAD_DIM * (j + 1))
            rhs.append(jnp.where(in_head, xdt_b, jnp.zeros_like(xdt_b)))
        y_diag = _dot(jnp.concatenate(lhs, axis=1), jnp.concatenate(rhs, axis=0))
        rows = slice(g * W, (g + 1) * W)
        h_prev = h_ref[rows, :]
        y_off = _dot_nt(cm_g, h_prev.astype(BF16)) * _expand_group(from_start, g, lane_g)
        xw = (xdt * _expand_group(to_end, g, lane_g)).astype(BF16)
        st = lax.dot_general(xw, bm_g, _TN, preferred_element_type=F32)
        for j in range(SSM_HEADS_PER_GROUP):
            hh = SSM_HEADS_PER_GROUP * g + j
            hr = slice(j * SSM_HEAD_DIM, (j + 1) * SSM_HEAD_DIM)
            h_ref[g * W + j * SSM_HEAD_DIM:g * W + (j + 1) * SSM_HEAD_DIM, :] = (
                h_prev[hr, :] * chunk_decay[hh:hh + 1, :] + st[hr, :])
        return y_diag + y_off + dskip_ref[:, rows] * xs_g

    def attend_head(h):
        hs = slice(h * MEM_HEAD_DIM, (h + 1) * MEM_HEAD_DIM)
        return _attend(q[:, hs].astype(BF16), k_ref[:, hs], v_ref[:, hs])

    for j in range(n_blocks):
        wide_block(j)
        conv_block(j)
        if j < len(POOL_WINDOWS):
            pool_group(j)
    y_groups, y_mem = [], []
    for g in range(SSM_GROUPS):
        y_groups.append(ssd_group(g))
        y_mem.append(attend_head(g))
    y_pool = jnp.concatenate(y_pool, axis=1) * pscale_ref[...]
    poolst_ref[...] = u[T - POOL_BUF:, :]
    pbuf_ref[0:P0, :] = u[T - P0:, :]
    n_gate = N_BRANCH * D_MODEL // (2 * W)
    gates = jnp.concatenate(wide_out[:n_gate], axis=1)
    silu_z = jnp.concatenate(wide_out[n_gate:], axis=1)

    xo_ref[...] = _merge_tail(x, silu_z, gates,
                              jnp.concatenate(y_groups, axis=1), y_pool, jnp.concatenate(y_mem, axis=1),
                              snorm_ref[...], wbp_ref, wbs_ref, wbm_ref, wo_ref)


def _prompt_mixer_call(x, layer, batch, seq, p, kb, vb):
    T = PROMPT_CHUNK
    n_chunks = seq // T
    n_mem = kb.shape[2]
    x_spec = pl.BlockSpec((T, D_MODEL), lambda b, c: (b * n_chunks + c, 0))

    def wspec(rows, cols):
        return _const_spec((None, rows, cols), (layer, 0, 0))

    def vspec(cols):
        return _const_spec((None, 1, cols), (layer, 0, 0))

    kv_spec = pl.BlockSpec((None, None, n_mem, MEM_DIM), lambda b, c: (layer, b, 0, 0))
    in_specs = [x_spec, vspec(D_MODEL),
                wspec(D_MODEL, POOL_DIM), wspec(D_MODEL, SSM_INNER), wspec(D_MODEL, CONV_DIM),
                wspec(D_MODEL, LANES), wspec(D_MODEL, MEM_DIM), wspec(D_MODEL, N_BRANCH * D_MODEL),
                vspec(N_BRANCH * D_MODEL),
                _const_spec((None, 4, POOL_GROUP_DIM, POOL_GROUP_DIM), (layer, 0, 0, 0)),
                vspec(POOL_DIM), wspec(CONV_WIDTH, CONV_DIM), vspec(CONV_DIM), vspec(LANES), vspec(LANES),
                vspec(SSM_INNER), vspec(SSM_INNER),
                wspec(POOL_DIM, D_MODEL), wspec(SSM_INNER, D_MODEL), wspec(MEM_DIM, D_MODEL),
                wspec(D_MODEL, D_MODEL), kv_spec, kv_spec]
    out_shape = (jax.ShapeDtypeStruct(x.shape, F32),
                 jax.ShapeDtypeStruct((batch, POOL_BUF, POOL_DIM), F32),
                 jax.ShapeDtypeStruct((batch, CONV_WIDTH - 1, CONV_DIM), F32),
                 jax.ShapeDtypeStruct((batch, SSM_INNER, SSM_STATE), F32))
    out_specs = (x_spec,
                 pl.BlockSpec((None, POOL_BUF, POOL_DIM), lambda b, c: (b, 0, 0)),
                 pl.BlockSpec((None, CONV_WIDTH - 1, CONV_DIM), lambda b, c: (b, 0, 0)),
                 pl.BlockSpec((None, SSM_INNER, SSM_STATE), lambda b, c: (b, 0, 0)))
    return pl.pallas_call(
        _prompt_mixer_kernel,
        out_shape=out_shape,
        grid=(batch, n_chunks),
        in_specs=in_specs,
        out_specs=out_specs,
        scratch_shapes=[pltpu.VMEM((SUBLANES + T, CONV_DIM), F32),
                        pltpu.VMEM((2 * SUBLANES + T, POOL_DIM), F32)],
        input_output_aliases={0: 0},
        compiler_params=pltpu.CompilerParams(dimension_semantics=("arbitrary", "arbitrary"),
                                             vmem_limit_bytes=VMEM_LIMIT),
        name="prompt_mixer",
    )(x, p["mix_norm"], p["w_pool"], p["w_z"], p["w_xbc"], p["w_dt"], p["w_q"], p["w_g"],
      p["gate_bias"], p["pool_w"], p["pool_scale"], p["conv_w"], p["conv_b"], p["dt_bias"], p["a_log"],
      p["d_skip"], p["ssm_norm"], p["w_br_pool"], p["w_br_ssm"], p["w_br_mem"], p["w_o"], kb, vb)


def _sample_inproj_kernel(x_ref, norm_ref, wpool_ref, wz_ref, wxbc_ref, wdt_ref, wq_ref, wg_ref,
                          u_ref, z_ref, xbc_ref, dt_ref, q_ref, g_ref):
    hn = _rmsnorm(x_ref[...], norm_ref[...]).astype(BF16)
    u_ref[...] = _dot(hn, wpool_ref[...])
    z_ref[...] = _dot(hn, wz_ref[...])
    xbc_ref[...] = _dot(hn, wxbc_ref[...])
    dt_ref[...] = _dot(hn, wdt_ref[...])
    q_ref[...] = _dot(hn, wq_ref[...])
    g_ref[...] = _dot(hn, wg_ref[...])


def _sample_inproj_call(x, layer, row_block, n_rows, p):
    def wspec(cols):
        return _const_spec((None, D_MODEL, cols), (layer, 0, 0))

    widths = (POOL_DIM, SSM_INNER, CONV_DIM, LANES, MEM_DIM, N_BRANCH * D_MODEL)
    return pl.pallas_call(
        _sample_inproj_kernel,
        out_shape=tuple(jax.ShapeDtypeStruct((n_rows, w), F32) for w in widths),
        grid=(1,),
        in_specs=[pl.BlockSpec((n_rows, D_MODEL), lambda i: (row_block, 0)),
                  _const_spec((None, 1, D_MODEL), (layer, 0, 0)),
                  wspec(POOL_DIM), wspec(SSM_INNER), wspec(CONV_DIM), wspec(LANES), wspec(MEM_DIM),
                  wspec(N_BRANCH * D_MODEL)],
        out_specs=tuple(pl.BlockSpec((n_rows, w), lambda i: (0, 0)) for w in widths),
        compiler_params=pltpu.CompilerParams(dimension_semantics=("arbitrary",),
                                             vmem_limit_bytes=VMEM_LIMIT),
        name="sample_inproj",
    )(x, p["mix_norm"], p["w_pool"], p["w_z"], p["w_xbc"], p["w_dt"], p["w_q"], p["w_g"])


def _sample_ssd_kernel(cext_ref, pext_ref, dtext_ref, st_ref, convw_ref, convb_ref, dtb_ref, alog_ref,
                       dskip_ref, poolw_ref, pscale_ref,
                       y_ref, ypool_ref, stn_ref, yoff_ref):
    R = SAMPLE_SEQ_TILE * SEQ_ROWS
    n_tok = SEQ_ROWS // 2
    row = lax.broadcasted_iota(jnp.int32, (R, 1), 0)
    t_idx = row % SEQ_ROWS - n_tok
    tok = t_idx >= 0

    f = pext_ref[...]
    sums = []
    s = f
    for step in (1, 2, 4, 8):
        s = s + pltpu.roll(s, step, 0)
        sums.append(s)
    y_parts = []
    for g, w in enumerate(POOL_WINDOWS):
        cs = slice(g * POOL_GROUP_DIM, (g + 1) * POOL_GROUP_DIM)
        d = (sums[g][:, cs] * (1.0 / w) - f[:, cs]).astype(BF16)
        y_parts.append(_dot(d, poolw_ref[g]))
    ypool_ref[...] = jnp.concatenate(y_parts, axis=1) * pscale_ref[...]

    ext = cext_ref[...]
    acc = convb_ref[...] + convw_ref[CONV_WIDTH - 1:CONV_WIDTH, :] * ext
    for k in range(CONV_WIDTH - 1):
        acc = acc + convw_ref[k:k + 1, :] * pltpu.roll(ext, CONV_WIDTH - 1 - k, 0)
    xc = jnp.where(tok, _silu(acc), 0.0)
    xs = xc[:, :SSM_INNER]
    bm = xc[:, SSM_INNER:SSM_INNER + SSM_GROUPS * SSM_STATE]
    cm = xc[:, SSM_INNER + SSM_GROUPS * SSM_STATE:]

    dt = jnp.where(tok, _softplus(dtext_ref[...] + dtb_ref[...]), 0.0)
    la = dt * (-jnp.exp(alog_ref[...]))
    c1 = la + pltpu.roll(la, 1, 0)
    acum = c1 + pltpu.roll(c1, 2, 0)
    nxt = pltpu.roll(la, R - 1, 0)
    n1 = nxt + pltpu.roll(nxt, R - 1, 0)
    rest = n1 + pltpu.roll(n1, R - 2, 0)
    lane_g = lax.broadcasted_iota(jnp.int32, (R, SSM_GROUP_DIM), 1)
    lane_h = lax.broadcasted_iota(jnp.int32, (R, LANES), 1)

    xdt = xs * _expand_heads(dt, lane_g)
    xdt_q = xdt.astype(BF16).astype(F32)
    cm_q = cm.astype(BF16).astype(F32)
    bm_q = bm.astype(BF16).astype(F32)

    y_diag = jnp.zeros((R, SSM_INNER), F32)
    for k in range(n_tok):
        b_k = bm_q if k == 0 else pltpu.roll(bm_q, k, 0)
        prod = cm_q * b_k
        cb = jnp.zeros((R, LANES), F32)
        for g in reversed(range(SSM_GROUPS)):
            s_g = jnp.sum(prod[:, g * SSM_STATE:(g + 1) * SSM_STATE], axis=-1, keepdims=True)
            cb = jnp.where(lane_h < SSM_HEADS_PER_GROUP * (g + 1), s_g, cb)
        a_k = acum if k == 0 else pltpu.roll(acum, k, 0)
        decay = jnp.exp(jnp.where(t_idx >= k, acum - a_k, -jnp.inf))
        coef = (cb * decay).astype(BF16).astype(F32)
        x_k = xdt_q if k == 0 else pltpu.roll(xdt_q, k, 0)
        y_diag = y_diag + _expand_heads(coef, lane_g) * x_k

    xw_t = (xdt * _expand_heads(jnp.exp(rest), lane_g)).T
    total_t = jnp.exp(acum + rest).T
    bm_b = bm.astype(BF16)
    col = lax.broadcasted_iota(jnp.int32, (SSM_GROUP_DIM, R), 1)
    for b in range(SAMPLE_SEQ_TILE):
        r0 = b * SEQ_ROWS
        c_rows = cm[r0:r0 + SEQ_ROWS, :]
        in_seq = (col >= r0) & (col < r0 + SEQ_ROWS)
        last_col = r0 + SEQ_ROWS - 1
        for g in range(SSM_GROUPS):
            rows = slice(g * SSM_GROUP_DIM, (g + 1) * SSM_GROUP_DIM)
            h0 = st_ref[b, rows, :]
            c_g = c_rows[:, g * SSM_STATE:(g + 1) * SSM_STATE].astype(BF16)
            yoff_ref[r0:r0 + SEQ_ROWS, rows] = _dot_nt(c_g, h0.astype(BF16))
            lhs = jnp.where(in_seq, xw_t[rows, :], 0.0).astype(BF16)
            st = _dot(lhs, bm_b[:, g * SSM_STATE:(g + 1) * SSM_STATE])
            for j in range(SSM_HEADS_PER_GROUP):
                hh = SSM_HEADS_PER_GROUP * g + j
                hr = slice(j * SSM_HEAD_DIM, (j + 1) * SSM_HEAD_DIM)
                stn_ref[b, g * SSM_GROUP_DIM + j * SSM_HEAD_DIM:g * SSM_GROUP_DIM + (j + 1) * SSM_HEAD_DIM, :] = (
                    h0[hr, :] * total_t[hh:hh + 1, last_col:last_col + 1] + st[hr, :])

    y_ref[...] = (y_diag + yoff_ref[...] * _expand_heads(jnp.exp(acum), lane_g) + dskip_ref[...] * xs)


def _sample_ssd_chained_kernel(cext_ref, pext_ref, dtext_ref, st_ref, convw_ref, convb_ref, dtb_ref, alog_ref,
                              dskip_ref, poolw_ref, pscale_ref, prev_ref, *rest):
    del prev_ref
    _sample_ssd_kernel(cext_ref, pext_ref, dtext_ref, st_ref, convw_ref, convb_ref, dtb_ref, alog_ref,
                       dskip_ref, poolw_ref, pscale_ref, *rest)


def _sample_ssd_call(layer, cext, pext, dtext, state, new_state, p):
    n_seq = state.shape[1]
    R = SAMPLE_SEQ_TILE * SEQ_ROWS
    RP = SAMPLE_SEQ_TILE * POOL_SEQ_ROWS

    def vspec(rows, cols):
        return _const_spec((None, rows, cols), (layer, 0, 0))

    st_spec = pl.BlockSpec((None, SAMPLE_SEQ_TILE, SSM_INNER, SSM_STATE), lambda i: (layer, i, 0, 0))
    in_specs = [pl.BlockSpec((R, CONV_DIM), lambda i: (i, 0)),
                pl.BlockSpec((RP, POOL_DIM), lambda i: (i, 0)),
                pl.BlockSpec((R, LANES), lambda i: (i, 0)),
                st_spec,
                vspec(CONV_WIDTH, CONV_DIM), vspec(1, CONV_DIM), vspec(1, LANES), vspec(1, LANES),
                vspec(1, SSM_INNER),
                _const_spec((None, 4, POOL_GROUP_DIM, POOL_GROUP_DIM), (layer, 0, 0, 0)),
                vspec(1, POOL_DIM)]
    args = [cext, pext, dtext, state, p["conv_w"], p["conv_b"], p["dt_bias"], p["a_log"], p["d_skip"],
            p["pool_w"], p["pool_scale"]]
    kern, aliases = _sample_ssd_kernel, {}
    if new_state is not None:
        in_specs.append(pl.BlockSpec(memory_space=pl.ANY))
        args.append(new_state)
        kern, aliases = _sample_ssd_chained_kernel, {len(args) - 1: 2}
    return pl.pallas_call(
        kern,
        out_shape=(jax.ShapeDtypeStruct((n_seq * SEQ_ROWS, SSM_INNER), F32),
                   jax.ShapeDtypeStruct((n_seq * POOL_SEQ_ROWS, POOL_DIM), F32),
                   jax.ShapeDtypeStruct(state.shape, F32)),
        grid=(n_seq // SAMPLE_SEQ_TILE,),
        in_specs=in_specs,
        out_specs=(pl.BlockSpec((R, SSM_INNER), lambda i: (i, 0)),
                   pl.BlockSpec((RP, POOL_DIM), lambda i: (i, 0)),
                   st_spec),
        scratch_shapes=[pltpu.VMEM((R, SSM_INNER), F32)],
        input_output_aliases=aliases,
        compiler_params=pltpu.CompilerParams(dimension_semantics=("arbitrary",),
                                             vmem_limit_bytes=VMEM_LIMIT),
        name="sample_ssd",
    )(*args)


def _sample_attn_kernel(q_ref, k_ref, v_ref, o_ref, s_ref):
    n_mem = k_ref.shape[1] // MEM_HEADS
    HR = MEM_HEADS * SEQ_ROWS
    row_head = lax.broadcasted_iota(jnp.int32, (HR, MEM_DIM), 0) // SEQ_ROWS
    col_head = lax.broadcasted_iota(jnp.int32, (HR, MEM_DIM), 1) // MEM_HEAD_DIM
    own_head = row_head == col_head

    def heads_on_lanes(ref, b):
        return jnp.concatenate([ref[b, pl.ds(h, n_mem, stride=MEM_HEADS), :] for h in range(MEM_HEADS)],
                               axis=1).astype(BF16)

    for b in range(SAMPLE_ATTN_TILE):
        q = q_ref[b * SEQ_ROWS:(b + 1) * SEQ_ROWS, :]
        q_heads = jnp.where(own_head, jnp.concatenate([q] * MEM_HEADS, axis=0), 0.0).astype(BF16)
        s_ref[b * HR:(b + 1) * HR, :] = _dot_nt(q_heads, heads_on_lanes(k_ref, b))
    s = s_ref[...] * (MEM_HEAD_DIM ** -0.5)
    e = jnp.exp(s - jnp.max(s, axis=-1, keepdims=True))
    p = (e * (1.0 / jnp.sum(e, axis=-1, keepdims=True))).astype(BF16)
    for b in range(SAMPLE_ATTN_TILE):
        o = _dot(p[b * HR:(b + 1) * HR, :], heads_on_lanes(v_ref, b))
        o_ref[b * SEQ_ROWS:(b + 1) * SEQ_ROWS, :] = jnp.concatenate(
            [o[h * SEQ_ROWS:(h + 1) * SEQ_ROWS, h * MEM_HEAD_DIM:(h + 1) * MEM_HEAD_DIM]
             for h in range(MEM_HEADS)], axis=1)


def _sample_attn_call(layer, qext, cache_k, cache_v):
    n_seq, kv_rows = cache_k.shape[1], cache_k.shape[2]
    R = SAMPLE_ATTN_TILE * SEQ_ROWS
    kv_spec = pl.BlockSpec((None, SAMPLE_ATTN_TILE, kv_rows, MEM_HEAD_DIM), lambda i: (layer, i, 0, 0))
    return pl.pallas_call(
        _sample_attn_kernel,
        out_shape=jax.ShapeDtypeStruct((n_seq * SEQ_ROWS, MEM_DIM), F32),
        grid=(n_seq // SAMPLE_ATTN_TILE,),
        in_specs=[pl.BlockSpec((R, MEM_DIM), lambda i: (i, 0)), kv_spec, kv_spec],
        out_specs=pl.BlockSpec((R, MEM_DIM), lambda i: (i, 0)),
        scratch_shapes=[pltpu.VMEM((SAMPLE_ATTN_TILE * MEM_HEADS * SEQ_ROWS, kv_rows // MEM_HEADS), F32)],
        compiler_params=pltpu.CompilerParams(dimension_semantics=("arbitrary",),
                                             vmem_limit_bytes=VMEM_LIMIT),
        name="sample_attn",
    )(qext, cache_k, cache_v)


def _sample_merge_kernel(x_ref, z_ref, g_ref, y_ref, ypool_ref, ymem_ref, gbias_ref, snorm_ref,
                         wbp_ref, wbs_ref, wbm_ref, wo_ref, xo_ref):
    xo_ref[...] = _merge_tail(x_ref[...], _silu(z_ref[...]), jax.nn.sigmoid(g_ref[...] + gbias_ref[...]),
                              y_ref[...], ypool_ref[...], ymem_ref[...], snorm_ref[...],
                              wbp_ref, wbs_ref, wbm_ref, wo_ref)


def _sample_merge_call(x, layer, row_block, n_rows, z, graw, y_raw, y_pool, y_mem, p):
    def full(cols):
        return pl.BlockSpec((n_rows, cols), lambda i: (0, 0))

    def wspec(rows, cols):
        return _const_spec((None, rows, cols), (layer, 0, 0))

    x_spec = pl.BlockSpec((n_rows, D_MODEL), lambda i: (row_block, 0))
    return pl.pallas_call(
        _sample_merge_kernel,
        out_shape=jax.ShapeDtypeStruct(x.shape, F32),
        grid=(1,),
        in_specs=[x_spec, full(SSM_INNER), full(N_BRANCH * D_MODEL), full(SSM_INNER), full(POOL_DIM),
                  full(MEM_DIM), wspec(1, N_BRANCH * D_MODEL), wspec(1, SSM_INNER),
                  wspec(POOL_DIM, D_MODEL), wspec(SSM_INNER, D_MODEL), wspec(MEM_DIM, D_MODEL),
                  wspec(D_MODEL, D_MODEL)],
        out_specs=x_spec,
        input_output_aliases={0: 0},
        compiler_params=pltpu.CompilerParams(dimension_semantics=("arbitrary",),
                                             vmem_limit_bytes=VMEM_LIMIT),
        name="sample_merge",
    )(x, z, graw, y_raw, y_pool, y_mem, p["gate_bias"], p["ssm_norm"],
      p["w_br_pool"], p["w_br_ssm"], p["w_br_mem"], p["w_o"])


def _row3(a):
    return a.reshape(a.shape[0], 1, a.shape[1])


def _pad_lanes(a):
    return jnp.pad(a, [(0, 0)] * (a.ndim - 1) + [(0, LANES - a.shape[-1])])


def _token_rows(ext, rows_per_seq, n_tok):
    n_seq = ext.shape[0] // rows_per_seq
    return ext.reshape(n_seq, rows_per_seq, -1)[:, rows_per_seq - n_tok:, :].reshape(n_seq * n_tok, -1)


def kernel(x_prompt, x_sample, mem_prompt, state_pool, state_conv, state_ssm, cache_mem_k, cache_mem_v,
           ffn1_norm, ffn1_w_gate, ffn1_w_up, ffn1_w_down, mix_norm, w_in, gate_bias,
           pool_w, pool_scale, conv_w, conv_b, dt_bias, a_log, d_skip, ssm_norm,
           mem_norm, w_mem_k, w_mem_v, w_br_pool, w_br_ssm, w_br_mem, w_o,
           ffn2_norm, ffn2_w_gate, ffn2_w_up, ffn2_w_down, final_norm):
    depth = w_in.shape[0]
    batch, seq, _ = x_prompt.shape
    n_seq, n_tok, _ = x_sample.shape
    n_mem = mem_prompt.shape[1]
    n_prompt = batch * seq
    n_sample = n_seq * n_tok
    assert n_sample == FFN_TOKEN_TILE and n_prompt % FFN_TOKEN_TILE == 0 and seq % PROMPT_CHUNK == 0
    assert n_tok == SEQ_ROWS // 2 and PAST_LEN >= max(POOL_WINDOWS)
    assert n_seq % SAMPLE_SEQ_TILE == 0 and n_seq % SAMPLE_ATTN_TILE == 0
    sample_block = n_prompt // n_sample

    cuts = [0]
    for w in IN_SPLITS:
        cuts.append(cuts[-1] + w)

    def w_in_cols(i):
        return w_in[:, :, cuts[i]:cuts[i + 1]].astype(BF16)

    params = dict(
        mix_norm=_row3(mix_norm),
        w_pool=w_in_cols(0), w_z=w_in_cols(1), w_xbc=w_in_cols(2), w_dt=_pad_lanes(w_in_cols(3)),
        w_q=w_in_cols(4), w_g=w_in_cols(5),
        gate_bias=_row3(gate_bias), pool_w=pool_w.astype(BF16), pool_scale=_row3(pool_scale),
        conv_w=conv_w, conv_b=_row3(conv_b), dt_bias=_row3(_pad_lanes(dt_bias)), a_log=_row3(_pad_lanes(a_log)),
        d_skip=_row3(jnp.repeat(d_skip, SSM_HEAD_DIM, axis=1)), ssm_norm=_row3(ssm_norm),
        w_br_pool=w_br_pool.astype(BF16), w_br_ssm=w_br_ssm.astype(BF16), w_br_mem=w_br_mem.astype(BF16),
        w_o=w_o.astype(BF16))
    ffn1 = (_row3(ffn1_norm), ffn1_w_gate.astype(BF16), ffn1_w_up.astype(BF16), ffn1_w_down.astype(BF16))
    ffn2 = (_row3(ffn2_norm), ffn2_w_gate.astype(BF16), ffn2_w_up.astype(BF16), ffn2_w_down.astype(BF16))

    mem_k, mem_v, mem_kb, mem_vb = _memkv_call(mem_prompt, _row3(mem_norm), w_mem_k.astype(BF16),
                                               w_mem_v.astype(BF16))
    cache_k = cache_mem_k.reshape(depth, n_seq, n_mem * MEM_HEADS, MEM_HEAD_DIM)
    cache_v = cache_mem_v.reshape(depth, n_seq, n_mem * MEM_HEADS, MEM_HEAD_DIM)
    state_ssm_rows = state_ssm.reshape(depth, n_seq, SSM_INNER, SSM_STATE)

    pool_p, conv_p, ssm_p, pool_s, conv_s = [], [], [], [], []
    ssm_s = None
    for l in range(depth):
        if l == 0:
            x = _ffn_first_call(x_prompt.reshape(n_prompt, D_MODEL), x_sample.reshape(n_sample, D_MODEL), l, ffn1)
        else:
            x = _ffn_call(x, l, ffn1)
        x, npool, nconv, nssm = _prompt_mixer_call(x, l, batch, seq, params, mem_kb, mem_vb)
        pool_p.append(npool)
        conv_p.append(nconv)
        ssm_p.append(nssm.reshape(batch, SSM_HEADS, SSM_HEAD_DIM, SSM_STATE))

        u, z, xbc, dt_raw, q, graw = _sample_inproj_call(x, l, sample_block, n_sample, params)
        u3 = u.reshape(n_seq, n_tok, POOL_DIM)
        xbc3 = xbc.reshape(n_seq, n_tok, CONV_DIM)
        pext = jnp.concatenate([jnp.zeros((n_seq, POOL_SEQ_ROWS - POOL_BUF - n_tok, POOL_DIM), F32),
                                state_pool[l], u3], axis=1).reshape(n_seq * POOL_SEQ_ROWS, POOL_DIM)
        cext = jnp.concatenate([jnp.zeros((n_seq, SEQ_ROWS - CONV_WIDTH + 1 - n_tok, CONV_DIM), F32),
                                state_conv[l], xbc3], axis=1).reshape(n_seq * SEQ_ROWS, CONV_DIM)

        def pad_seq(a):
            a3 = a.reshape(n_seq, n_tok, a.shape[-1])
            return jnp.concatenate([jnp.zeros_like(a3), a3], axis=1).reshape(n_seq * SEQ_ROWS, a.shape[-1])

        y_ext, ypool_ext, ssm_s = _sample_ssd_call(l, cext, pext, pad_seq(dt_raw), state_ssm_rows, ssm_s, params)
        ymem_ext = _sample_attn_call(l, pad_seq(q), cache_k, cache_v)
        x = _sample_merge_call(x, l, sample_block, n_sample, z, graw,
                               _token_rows(y_ext, SEQ_ROWS, n_tok), _token_rows(ypool_ext, POOL_SEQ_ROWS, n_tok),
                               _token_rows(ymem_ext, SEQ_ROWS, n_tok), params)
        pool_s.append(jnp.concatenate([state_pool[l][:, n_tok:], u3], axis=1))
        conv_s.append(xbc3[:, n_tok - (CONV_WIDTH - 1):])

        if l < depth - 1:
            x = _ffn_call(x, l, ffn2)
    y_prompt, y_sample = _ffn_final_call(x, n_prompt, depth - 1, ffn2, final_norm.reshape(1, D_MODEL))

    mem_shape = (depth, batch, n_mem, MEM_HEADS, MEM_HEAD_DIM)
    return (y_prompt.reshape(batch, seq, D_MODEL), y_sample.reshape(n_seq, n_tok, D_MODEL),
            jnp.stack(pool_p), jnp.stack(conv_p), jnp.stack(ssm_p),
            mem_k.reshape(mem_shape), mem_v.reshape(mem_shape),
            jnp.stack(pool_s), jnp.stack(conv_s),
            ssm_s.reshape(depth, n_seq, SSM_HEADS, SSM_HEAD_DIM, SSM_STATE))
```

```python
import functools

import jax
import jax.numpy as jnp
from jax import lax
from jax.experimental import pallas as pl
from jax.experimental.pallas import tpu as pltpu

F32 = jnp.float32
BF16 = jnp.bfloat16

D_MODEL = 1024
D_FF = 2816
EPS = 1e-6
PAST_LEN = 16384

POOL_WINDOWS = (2, 4, 8, 16)
POOL_GROUP_DIM = 128
POOL_DIM = 512
POOL_BUF = 15
SSM_INNER = 1024
SSM_HEADS = 16
SSM_HEAD_DIM = 64
SSM_GROUPS = 4
SSM_HEADS_PER_GROUP = 4
SSM_GROUP_DIM = 256
SSM_STATE = 128
CONV_WIDTH = 4
CONV_DIM = 2048
MEM_HEADS = 4
MEM_HEAD_DIM = 128
MEM_DIM = 512
N_BRANCH = 3
IN_SPLITS = (POOL_DIM, SSM_INNER, CONV_DIM, SSM_HEADS, MEM_DIM, N_BRANCH * D_MODEL)

SUBLANES = 8
LANES = 128

FFN_TOKEN_TILE = 512
FFN_FF_CHUNK = 2816
PROMPT_CHUNK = 256
SAMPLE_SEQ_TILE = 16
SAMPLE_ATTN_TILE = 8
MEMKV_BATCH_TILE = 4
W_IN_ROW_TILE = 256
SEQ_ROWS = 8
POOL_SEQ_ROWS = 24
VMEM_LIMIT = 56 * 1024 * 1024

_NT = (((1,), (1,)), ((), ()))
_TN = (((0,), (0,)), ((), ()))


def _dot(a, b):
    return jnp.dot(a, b, preferred_element_type=F32)


def _dot_nt(a, b):
    return lax.dot_general(a, b, _NT, preferred_element_type=F32)


def _rmsnorm(x, g):
    return x * lax.rsqrt(jnp.mean(x * x, axis=-1, keepdims=True) + EPS) * g


def _silu(x):
    return x * jax.nn.sigmoid(x)


def _softplus(x):
    return jnp.maximum(x, 0.0) + jnp.log1p(jnp.exp(-jnp.abs(x)))


def _pitch(cols):
    return cols + LANES if cols % (SUBLANES * LANES) == 0 else cols


def _pad_pitch(w):
    extra = _pitch(w.shape[-1]) - w.shape[-1]
    return jnp.pad(w, [(0, 0)] * (w.ndim - 1) + [(0, extra)]) if extra else w


def _const_spec(shape, index, single_buffer=True):
    mode = pl.Buffered(1) if single_buffer else None
    return pl.BlockSpec(shape, lambda *_: index, pipeline_mode=mode)


def _expand_group(v, g, lane):
    h0 = SSM_HEADS_PER_GROUP * g
    out = jnp.broadcast_to(v[:, h0 + 3:h0 + 4], lane.shape)
    for j in (2, 1, 0):
        out = jnp.where(lane < SSM_HEAD_DIM * (j + 1), v[:, h0 + j:h0 + j + 1], out)
    return out


def _expand_heads(v, lane):
    return jnp.concatenate([_expand_group(v, g, lane) for g in range(SSM_GROUPS)], axis=1)


def _ffn_body(x, g, wg_ref, wu_ref, wd_ref):
    hn = _rmsnorm(x, g).astype(BF16)
    acc = None
    for j in range(D_FF // FFN_FF_CHUNK):
        sl = slice(j * FFN_FF_CHUNK, (j + 1) * FFN_FF_CHUNK)
        gate = _dot(hn, wg_ref[:, sl])
        up = _dot(hn, wu_ref[:, sl])
        part = _dot((_silu(gate) * up).astype(BF16), wd_ref[sl, :D_MODEL])
        acc = part if acc is None else acc + part
    return x + 0.5 * acc


def _ffn_kernel(x_ref, g_ref, wg_ref, wu_ref, wd_ref, o_ref):
    o_ref[...] = _ffn_body(x_ref[...], g_ref[...], wg_ref, wu_ref, wd_ref)


def _ffn_first_kernel(xp_ref, xs_ref, g_ref, wg_ref, wu_ref, wd_ref, o_ref, *, prompt_blocks):
    x = jnp.where(pl.program_id(0) < prompt_blocks, xp_ref[...], xs_ref[...])
    o_ref[...] = _ffn_body(x, g_ref[...], wg_ref, wu_ref, wd_ref)


def _ffn_final_kernel(x_ref, g_ref, wg_ref, wu_ref, wd_ref, fg_ref, yp_ref, ys_ref, *, prompt_blocks):
    y = _rmsnorm(_ffn_body(x_ref[...], g_ref[...], wg_ref, wu_ref, wd_ref), fg_ref[...])
    i = pl.program_id(0)

    @pl.when(i < prompt_blocks)
    def _():
        yp_ref[...] = y

    @pl.when(i == prompt_blocks)
    def _():
        ys_ref[...] = y


def _ffn_weight_specs(layer):
    return [_const_spec((None, 1, D_MODEL), (layer, 0, 0)),
            _const_spec((None, D_MODEL, D_FF), (layer, 0, 0)),
            _const_spec((None, D_MODEL, D_FF), (layer, 0, 0)),
            _const_spec((None, D_FF, _pitch(D_MODEL)), (layer, 0, 0))]


_FFN_PARAMS = pltpu.CompilerParams(dimension_semantics=("arbitrary",), vmem_limit_bytes=VMEM_LIMIT)


def _ffn_call(x, layer, weights):
    tile = pl.BlockSpec((FFN_TOKEN_TILE, D_MODEL), lambda i: (i, 0))
    return pl.pallas_call(
        _ffn_kernel,
        out_shape=jax.ShapeDtypeStruct(x.shape, F32),
        grid=(x.shape[0] // FFN_TOKEN_TILE,),
        in_specs=[tile] + _ffn_weight_specs(layer),
        out_specs=tile,
        input_output_aliases={0: 0},
        compiler_params=_FFN_PARAMS,
        name="ffn",
    )(x, *weights)


def _ffn_first_call(x_prompt, x_sample, layer, weights):
    prompt_blocks = x_prompt.shape[0] // FFN_TOKEN_TILE
    n_tok = x_prompt.shape[0] + x_sample.shape[0]
    tile = pl.BlockSpec((FFN_TOKEN_TILE, D_MODEL), lambda i: (i, 0))
    p_tile = pl.BlockSpec((FFN_TOKEN_TILE, D_MODEL), lambda i: (jnp.minimum(i, prompt_blocks - 1), 0))
    s_tile = pl.BlockSpec((FFN_TOKEN_TILE, D_MODEL), lambda i: (0, 0))
    return pl.pallas_call(
        functools.partial(_ffn_first_kernel, prompt_blocks=prompt_blocks),
        out_shape=jax.ShapeDtypeStruct((n_tok, D_MODEL), F32),
        grid=(n_tok // FFN_TOKEN_TILE,),
        in_specs=[p_tile, s_tile] + _ffn_weight_specs(layer),
        out_specs=tile,
        compiler_params=_FFN_PARAMS,
        name="ffn_first",
    )(x_prompt, x_sample, *weights)


def _ffn_final_call(x, n_prompt, layer, weights, final_norm):
    prompt_blocks = n_prompt // FFN_TOKEN_TILE
    n_tok = x.shape[0]
    tile = pl.BlockSpec((FFN_TOKEN_TILE, D_MODEL), lambda i: (i, 0))
    p_tile = pl.BlockSpec((FFN_TOKEN_TILE, D_MODEL), lambda i: (jnp.minimum(i, prompt_blocks - 1), 0))
    s_tile = pl.BlockSpec((FFN_TOKEN_TILE, D_MODEL), lambda i: (0, 0))
    return pl.pallas_call(
        functools.partial(_ffn_final_kernel, prompt_blocks=prompt_blocks),
        out_shape=(jax.ShapeDtypeStruct((n_prompt, D_MODEL), F32),
                   jax.ShapeDtypeStruct((n_tok - n_prompt, D_MODEL), F32)),
        grid=(n_tok // FFN_TOKEN_TILE,),
        in_specs=[tile] + _ffn_weight_specs(layer) + [_const_spec((1, D_MODEL), (0, 0))],
        out_specs=(p_tile, s_tile),
        compiler_params=_FFN_PARAMS,
        name="ffn_final",
    )(x, *weights, final_norm)


def _split_w_in_kernel(w_ref, *out_refs):
    w = w_ref[...]
    start = 0
    for o_ref, cols in zip(out_refs, IN_SPLITS):
        stored = o_ref.shape[-1]
        if cols >= LANES:
            o_ref[:, :cols] = w[:, start:start + cols].astype(BF16)
            if stored > cols:
                o_ref[:, cols:] = jnp.zeros((w.shape[0], stored - cols), BF16)
        else:
            tile = w[:, start:start + stored]
            lane = lax.broadcasted_iota(jnp.int32, tile.shape, 1)
            o_ref[...] = jnp.where(lane < cols, tile, 0.0).astype(BF16)
        start += cols


def _split_w_in_call(w_in):
    depth, rows, total = w_in.shape
    stored = [_pitch(c) if c >= LANES else LANES for c in IN_SPLITS]
    tile = W_IN_ROW_TILE
    return pl.pallas_call(
        _split_w_in_kernel,
        out_shape=tuple(jax.ShapeDtypeStruct((depth, rows, s), BF16) for s in stored),
        grid=(depth, rows // tile),
        in_specs=[pl.BlockSpec((None, tile, total), lambda l, r: (l, r, 0))],
        out_specs=tuple(pl.BlockSpec((None, tile, s), lambda l, r: (l, r, 0)) for s in stored),
        compiler_params=pltpu.CompilerParams(dimension_semantics=("arbitrary", "arbitrary"),
                                             vmem_limit_bytes=VMEM_LIMIT),
        name="split_w_in",
    )(w_in)


def _memkv_kernel(m_ref, g_ref, wk_ref, wv_ref, k_ref, v_ref, kb_ref, vb_ref):
    nb, n_mem, _ = m_ref.shape
    hn = _rmsnorm(m_ref[...].reshape(nb * n_mem, D_MODEL), g_ref[...]).astype(BF16)
    k = _dot(hn, wk_ref[...])
    v = _dot(hn, wv_ref[...])
    for b in range(nb):
        rows = slice(b * n_mem, (b + 1) * n_mem)
        for h in range(MEM_HEADS):
            hs = slice(h * MEM_HEAD_DIM, (h + 1) * MEM_HEAD_DIM)
            k_ref[b, pl.ds(h, n_mem, stride=MEM_HEADS), :] = k[rows, hs]
            v_ref[b, pl.ds(h, n_mem, stride=MEM_HEADS), :] = v[rows, hs]
        kb_ref[b] = k[rows, :].astype(BF16)
        vb_ref[b] = v[rows, :].astype(BF16)


def _memkv_call(mem, norm, wk, wv):
    depth = norm.shape[0]
    batch, n_mem, _ = mem.shape
    nb = MEMKV_BATCH_TILE
    f32_spec = pl.BlockSpec((None, nb, n_mem * MEM_HEADS, MEM_HEAD_DIM), lambda l, b: (l, b, 0, 0))
    out_spec = pl.BlockSpec((None, nb, n_mem, MEM_DIM), lambda l, b: (l, b, 0, 0))
    w_spec = pl.BlockSpec((None, D_MODEL, MEM_DIM), lambda l, b: (l, 0, 0))
    f32_out = jax.ShapeDtypeStruct((depth, batch, n_mem * MEM_HEADS, MEM_HEAD_DIM), F32)
    bf16_out = jax.ShapeDtypeStruct((depth, batch, n_mem, MEM_DIM), BF16)
    return pl.pallas_call(
        _memkv_kernel,
        out_shape=(f32_out, f32_out, bf16_out, bf16_out),
        grid=(depth, batch // nb),
        in_specs=[pl.BlockSpec((nb, n_mem, D_MODEL), lambda l, b: (b, 0, 0)),
                  pl.BlockSpec((None, 1, D_MODEL), lambda l, b: (l, 0, 0)),
                  w_spec, w_spec],
        out_specs=(f32_spec, f32_spec, out_spec, out_spec),
        compiler_params=pltpu.CompilerParams(dimension_semantics=("arbitrary", "arbitrary"),
                                             vmem_limit_bytes=VMEM_LIMIT),
        name="mem_kv",
    )(mem, norm, wk, wv)


def _merge_tail(x, silu_z, gates, y_raw, y_pool, y_mem, snorm, wbp_ref, wbs_ref, wbm_ref, wo_ref):
    yz = y_raw * silu_z
    parts = []
    for g in range(SSM_GROUPS):
        yg = yz[:, g * SSM_GROUP_DIM:(g + 1) * SSM_GROUP_DIM]
        parts.append(yg * lax.rsqrt(jnp.mean(yg * yg, axis=-1, keepdims=True) + EPS))
    y_ssm = (jnp.concatenate(parts, axis=1) * snorm).astype(BF16)
    merged = (gates[:, :D_MODEL] * _dot(y_pool.astype(BF16), wbp_ref[:, :D_MODEL])
              + gates[:, D_MODEL:2 * D_MODEL] * _dot(y_ssm, wbs_ref[:, :D_MODEL])
              + gates[:, 2 * D_MODEL:] * _dot(y_mem.astype(BF16), wbm_ref[:, :D_MODEL]))
    return x + _dot(merged.astype(BF16), wo_ref[:, :D_MODEL])


def _attend(q, k, v):
    s = _dot_nt(q, k) * (MEM_HEAD_DIM ** -0.5)
    e = jnp.exp(s - jnp.max(s, axis=-1, keepdims=True))
    p = e * (1.0 / jnp.sum(e, axis=-1, keepdims=True))
    return _dot(p.astype(BF16), v)


def _prompt_mixer_kernel(x_ref, norm_ref, wpool_ref, wz_ref, wxbc_ref, wdt_ref, wq_ref, wg_ref,
                         gbias_ref, poolw_ref, pscale_ref, convw_ref, convb_ref, dtb_ref, alog_ref,
                         dskip_ref, snorm_ref, wbp_ref, wbs_ref, wbm_ref, wo_ref, k_ref, v_ref,
                         xo_ref, poolst_ref, convst_ref, h_ref,
                         cbuf_ref, pbuf_ref):
    T = PROMPT_CHUNK
    W = SSM_GROUP_DIM
    c = pl.program_id(1)

    @pl.when(c == 0)
    def _():
        h_ref[...] = jnp.zeros_like(h_ref)
        cbuf_ref[0:SUBLANES, :] = jnp.zeros((SUBLANES, CONV_DIM), F32)
        pbuf_ref[0:2 * SUBLANES, :] = jnp.zeros((2 * SUBLANES, POOL_DIM), F32)

    x = x_ref[...]
    hn = _rmsnorm(x, norm_ref[...]).astype(BF16)

    dt = _softplus(_dot(hn, wdt_ref[...]) + dtb_ref[...])
    la = dt * (-jnp.exp(alog_ref[...]))
    la_t = la.T[0:SSM_HEADS, :]
    lane_t = lax.broadcasted_iota(jnp.int32, (SSM_HEADS, T), 1)
    acum_t = la_t
    shift = 1
    while shift < T:
        acum_t = acum_t + jnp.where(lane_t >= shift, pltpu.roll(acum_t, shift, 1), 0.0)
        shift *= 2
    acum = jnp.concatenate([acum_t, jnp.zeros((LANES - SSM_HEADS, T), F32)], axis=0).T
    from_start = jnp.exp(acum)
    to_end = jnp.exp(acum[T - 1:T, :] - acum)
    chunk_decay = jnp.exp(acum_t[:, T - 1:T])

    P0 = 2 * SUBLANES
    u = _dot(hn, wpool_ref[...])
    pbuf_ref[P0:P0 + T, :] = u
    pos = c * T + lax.broadcasted_iota(jnp.int32, (T, 1), 0)

    C0 = SUBLANES
    n_blocks = CONV_DIM // W
    n_x = SSM_INNER // W
    xc = [None] * n_blocks
    wide_out = [None] * n_blocks
    y_pool = [None] * len(POOL_WINDOWS)
    q = _dot(hn, wq_ref[...])
    lane_g = lax.broadcasted_iota(jnp.int32, (T, W), 1)
    causal = (lax.broadcasted_iota(jnp.int32, (T, T), 0) >= lax.broadcasted_iota(jnp.int32, (T, T), 1))

    def wide_block(i):
        lo, hi = 2 * i * W, 2 * (i + 1) * W
        if hi <= N_BRANCH * D_MODEL:
            wide_out[i] = jax.nn.sigmoid(_dot(hn, wg_ref[:, lo:hi]) + gbias_ref[:, lo:hi])
        else:
            lo, hi = lo - N_BRANCH * D_MODEL, hi - N_BRANCH * D_MODEL
            wide_out[i] = _silu(_dot(hn, wz_ref[:, lo:hi]))

    def conv_block(j):
        cols = slice(j * W, (j + 1) * W)
        xbc = _dot(hn, wxbc_ref[:, cols])
        cbuf_ref[C0:C0 + T, cols] = xbc
        acc = convb_ref[:, cols] + convw_ref[CONV_WIDTH - 1:CONV_WIDTH, cols] * xbc
        for k in range(CONV_WIDTH - 1):
            off = C0 - (CONV_WIDTH - 1) + k
            acc = acc + convw_ref[k:k + 1, cols] * cbuf_ref[off:off + T, cols]
        xc[j] = _silu(acc)
        convst_ref[:, cols] = xbc[T - (CONV_WIDTH - 1):, :]
        cbuf_ref[0:C0, cols] = xbc[T - C0:, :]

    def pool_group(j):
        w = POOL_WINDOWS[j]
        cs = slice(j * POOL_GROUP_DIM, (j + 1) * POOL_GROUP_DIM)
        ug = u[:, cs]
        s = ug
        for k in range(1, w):
            s = s + pbuf_ref[P0 - k:P0 - k + T, cs]
        inv_cnt = 1.0 / jnp.minimum(pos + 1, w).astype(F32)
        y_pool[j] = _dot((s * inv_cnt - ug).astype(BF16), poolw_ref[j])

    def ssd_group(g):
        half = slice((g % 2) * SSM_STATE, (g % 2 + 1) * SSM_STATE)
        xs_g = xc[g]
        bm_g = xc[n_x + g // 2][:, half].astype(BF16)
        cm_g = xc[n_x + SSM_GROUPS // 2 + g // 2][:, half].astype(BF16)
        cb = _dot_nt(cm_g, bm_g)
        xdt = xs_g * _expand_group(dt, g, lane_g)
        xdt_b = xdt.astype(BF16)
        lhs, rhs = [], []
        for j in range(SSM_HEADS_PER_GROUP):
            hh = SSM_HEADS_PER_GROUP * g + j
            seg = acum[:, hh:hh + 1] - acum_t[hh:hh + 1, :]
            decay = jnp.exp(jnp.where(causal, seg, -jnp.inf))
            lhs.append((cb * decay).astype(BF16))
            in_head = (lane_g >= SSM_HEAD_DIM * j) & (lane_g < SSM_HEAD_DIM * (j + 1))
            rhs.append(jnp.where(in_head, xdt_b, jnp.zeros_like(xdt_b)))
        y_diag = _dot(jnp.concatenate(lhs, axis=1), jnp.concatenate(rhs, axis=0))
        rows = slice(g * W, (g + 1) * W)
        h_prev = h_ref[rows, :]
        y_off = _dot_nt(cm_g, h_prev.astype(BF16)) * _expand_group(from_start, g, lane_g)
        xw = (xdt * _expand_group(to_end, g, lane_g)).astype(BF16)
        st = lax.dot_general(xw, bm_g, _TN, preferred_element_type=F32)
        for j in range(SSM_HEADS_PER_GROUP):
            hh = SSM_HEADS_PER_GROUP * g + j
            hr = slice(j * SSM_HEAD_DIM, (j + 1) * SSM_HEAD_DIM)
            h_ref[g * W + j * SSM_HEAD_DIM:g * W + (j + 1) * SSM_HEAD_DIM, :] = (
                h_prev[hr, :] * chunk_decay[hh:hh + 1, :] + st[hr, :])
        return y_diag + y_off + dskip_ref[:, rows] * xs_g

    def attend_head(h):
        hs = slice(h * MEM_HEAD_DIM, (h + 1) * MEM_HEAD_DIM)
        return _attend(q[:, hs].astype(BF16), k_ref[:, hs], v_ref[:, hs])

    for j in range(n_blocks):
        wide_block(j)
        conv_block(j)
        if j < len(POOL_WINDOWS):
            pool_group(j)
    y_groups, y_mem = [], []
    for g in range(SSM_GROUPS):
        y_groups.append(ssd_group(g))
        y_mem.append(attend_head(g))
    y_pool = jnp.concatenate(y_pool, axis=1) * pscale_ref[...]
    poolst_ref[...] = u[T - POOL_BUF:, :]
    pbuf_ref[0:P0, :] = u[T - P0:, :]
    n_gate = N_BRANCH * D_MODEL // (2 * W)
    gates = jnp.concatenate(wide_out[:n_gate], axis=1)
    silu_z = jnp.concatenate(wide_out[n_gate:], axis=1)

    xo_ref[...] = _merge_tail(x, silu_z, gates,
                              jnp.concatenate(y_groups, axis=1), y_pool, jnp.concatenate(y_mem, axis=1),
                              snorm_ref[...], wbp_ref, wbs_ref, wbm_ref, wo_ref)


def _prompt_mixer_call(x, layer, batch, seq, p, kb, vb):
    T = PROMPT_CHUNK
    n_chunks = seq // T
    n_mem = kb.shape[2]
    x_spec = pl.BlockSpec((T, D_MODEL), lambda b, c: (b * n_chunks + c, 0))

    def wspec(rows, cols):
        return _const_spec((None, rows, cols), (layer, 0, 0))

    def vspec(cols):
        return _const_spec((None, 1, cols), (layer, 0, 0))

    kv_spec = pl.BlockSpec((None, None, n_mem, MEM_DIM), lambda b, c: (layer, b, 0, 0))
    in_specs = [x_spec, vspec(D_MODEL),
                wspec(D_MODEL, POOL_DIM), wspec(D_MODEL, _pitch(SSM_INNER)), wspec(D_MODEL, _pitch(CONV_DIM)),
                wspec(D_MODEL, LANES), wspec(D_MODEL, MEM_DIM), wspec(D_MODEL, _pitch(N_BRANCH * D_MODEL)),
                vspec(N_BRANCH * D_MODEL),
                _const_spec((None, 4, POOL_GROUP_DIM, POOL_GROUP_DIM), (layer, 0, 0, 0)),
                vspec(POOL_DIM), wspec(CONV_WIDTH, CONV_DIM), vspec(CONV_DIM), vspec(LANES), vspec(LANES),
                vspec(SSM_INNER), vspec(SSM_INNER),
                wspec(POOL_DIM, _pitch(D_MODEL)), wspec(SSM_INNER, _pitch(D_MODEL)), wspec(MEM_DIM, _pitch(D_MODEL)),
                wspec(D_MODEL, _pitch(D_MODEL)), kv_spec, kv_spec]
    out_shape = (jax.ShapeDtypeStruct(x.shape, F32),
                 jax.ShapeDtypeStruct((batch, POOL_BUF, POOL_DIM), F32),
                 jax.ShapeDtypeStruct((batch, CONV_WIDTH - 1, CONV_DIM), F32),
                 jax.ShapeDtypeStruct((batch, SSM_INNER, SSM_STATE), F32))
    out_specs = (x_spec,
                 pl.BlockSpec((None, POOL_BUF, POOL_DIM), lambda b, c: (b, 0, 0)),
                 pl.BlockSpec((None, CONV_WIDTH - 1, CONV_DIM), lambda b, c: (b, 0, 0)),
                 pl.BlockSpec((None, SSM_INNER, SSM_STATE), lambda b, c: (b, 0, 0)))
    return pl.pallas_call(
        _prompt_mixer_kernel,
        out_shape=out_shape,
        grid=(batch, n_chunks),
        in_specs=in_specs,
        out_specs=out_specs,
        scratch_shapes=[pltpu.VMEM((SUBLANES + T, CONV_DIM), F32),
                        pltpu.VMEM((2 * SUBLANES + T, POOL_DIM), F32)],
        input_output_aliases={0: 0},
        compiler_params=pltpu.CompilerParams(dimension_semantics=("arbitrary", "arbitrary"),
                                             vmem_limit_bytes=VMEM_LIMIT),
        name="prompt_mixer",
    )(x, p["mix_norm"], p["w_pool"], p["w_z"], p["w_xbc"], p["w_dt"], p["w_q"], p["w_g"],
      p["gate_bias"], p["pool_w"], p["pool_scale"], p["conv_w"], p["conv_b"], p["dt_bias"], p["a_log"],
      p["d_skip"], p["ssm_norm"], p["w_br_pool"], p["w_br_ssm"], p["w_br_mem"], p["w_o"], kb, vb)


def _sample_inproj_kernel(x_ref, norm_ref, wpool_ref, wz_ref, wxbc_ref, wdt_ref, wq_ref, wg_ref,
                          u_ref, z_ref, xbc_ref, dt_ref, q_ref, g_ref):
    hn = _rmsnorm(x_ref[...], norm_ref[...]).astype(BF16)
    u_ref[...] = _dot(hn, wpool_ref[...])
    z_ref[...] = _dot(hn, wz_ref[:, :SSM_INNER])
    xbc_ref[...] = _dot(hn, wxbc_ref[:, :CONV_DIM])
    dt_ref[...] = _dot(hn, wdt_ref[...])
    q_ref[...] = _dot(hn, wq_ref[...])
    g_ref[...] = _dot(hn, wg_ref[:, :N_BRANCH * D_MODEL])


def _sample_inproj_call(x, layer, row_block, n_rows, p):
    def wspec(cols):
        return _const_spec((None, D_MODEL, cols), (layer, 0, 0))

    widths = (POOL_DIM, SSM_INNER, CONV_DIM, LANES, MEM_DIM, N_BRANCH * D_MODEL)
    return pl.pallas_call(
        _sample_inproj_kernel,
        out_shape=tuple(jax.ShapeDtypeStruct((n_rows, w), F32) for w in widths),
        grid=(1,),
        in_specs=[pl.BlockSpec((n_rows, D_MODEL), lambda i: (row_block, 0)),
                  _const_spec((None, 1, D_MODEL), (layer, 0, 0)),
                  wspec(POOL_DIM), wspec(_pitch(SSM_INNER)), wspec(_pitch(CONV_DIM)), wspec(LANES), wspec(MEM_DIM),
                  wspec(_pitch(N_BRANCH * D_MODEL))],
        out_specs=tuple(pl.BlockSpec((n_rows, w), lambda i: (0, 0)) for w in widths),
        compiler_params=pltpu.CompilerParams(dimension_semantics=("arbitrary",),
                                             vmem_limit_bytes=VMEM_LIMIT),
        name="sample_inproj",
    )(x, p["mix_norm"], p["w_pool"], p["w_z"], p["w_xbc"], p["w_dt"], p["w_q"], p["w_g"])


def _sample_ssd_kernel(cext_ref, pext_ref, dtext_ref, st_ref, convw_ref, convb_ref, dtb_ref, alog_ref,
                       dskip_ref, poolw_ref, pscale_ref,
                       y_ref, ypool_ref, stn_ref, yoff_ref):
    R = SAMPLE_SEQ_TILE * SEQ_ROWS
    n_tok = SEQ_ROWS // 2
    row = lax.broadcasted_iota(jnp.int32, (R, 1), 0)
    t_idx = row % SEQ_ROWS - n_tok
    tok = t_idx >= 0

    f = pext_ref[...]
    sums = []
    s = f
    for step in (1, 2, 4, 8):
        s = s + pltpu.roll(s, step, 0)
        sums.append(s)
    y_parts = []
    for g, w in enumerate(POOL_WINDOWS):
        cs = slice(g * POOL_GROUP_DIM, (g + 1) * POOL_GROUP_DIM)
        d = (sums[g][:, cs] * (1.0 / w) - f[:, cs]).astype(BF16)
        y_parts.append(_dot(d, poolw_ref[g]))
    ypool_ref[...] = jnp.concatenate(y_parts, axis=1) * pscale_ref[...]

    ext = cext_ref[...]
    acc = convb_ref[...] + convw_ref[CONV_WIDTH - 1:CONV_WIDTH, :] * ext
    for k in range(CONV_WIDTH - 1):
        acc = acc + convw_ref[k:k + 1, :] * pltpu.roll(ext, CONV_WIDTH - 1 - k, 0)
    xc = jnp.where(tok, _silu(acc), 0.0)
    xs = xc[:, :SSM_INNER]
    bm = xc[:, SSM_INNER:SSM_INNER + SSM_GROUPS * SSM_STATE]
    cm = xc[:, SSM_INNER + SSM_GROUPS * SSM_STATE:]

    dt = jnp.where(tok, _softplus(dtext_ref[...] + dtb_ref[...]), 0.0)
    la = dt * (-jnp.exp(alog_ref[...]))
    c1 = la + pltpu.roll(la, 1, 0)
    acum = c1 + pltpu.roll(c1, 2, 0)
    nxt = pltpu.roll(la, R - 1, 0)
    n1 = nxt + pltpu.roll(nxt, R - 1, 0)
    rest = n1 + pltpu.roll(n1, R - 2, 0)
    lane_g = lax.broadcasted_iota(jnp.int32, (R, SSM_GROUP_DIM), 1)
    lane_h = lax.broadcasted_iota(jnp.int32, (R, LANES), 1)

    xdt = xs * _expand_heads(dt, lane_g)
    xdt_q = xdt.astype(BF16).astype(F32)
    cm_q = cm.astype(BF16).astype(F32)
    bm_q = bm.astype(BF16).astype(F32)

    y_diag = jnp.zeros((R, SSM_INNER), F32)
    for k in range(n_tok):
        b_k = bm_q if k == 0 else pltpu.roll(bm_q, k, 0)
        prod = cm_q * b_k
        cb = jnp.zeros((R, LANES), F32)
        for g in reversed(range(SSM_GROUPS)):
            s_g = jnp.sum(prod[:, g * SSM_STATE:(g + 1) * SSM_STATE], axis=-1, keepdims=True)
            cb = jnp.where(lane_h < SSM_HEADS_PER_GROUP * (g + 1), s_g, cb)
        a_k = acum if k == 0 else pltpu.roll(acum, k, 0)
        decay = jnp.exp(jnp.where(t_idx >= k, acum - a_k, -jnp.inf))
        coef = (cb * decay).astype(BF16).astype(F32)
        x_k = xdt_q if k == 0 else pltpu.roll(xdt_q, k, 0)
        y_diag = y_diag + _expand_heads(coef, lane_g) * x_k

    xw_t = (xdt * _expand_heads(jnp.exp(rest), lane_g)).T
    total_t = jnp.exp(acum + rest).T
    bm_b = bm.astype(BF16)
    col = lax.broadcasted_iota(jnp.int32, (SSM_GROUP_DIM, R), 1)
    for b in range(SAMPLE_SEQ_TILE):
        r0 = b * SEQ_ROWS
        c_rows = cm[r0:r0 + SEQ_ROWS, :]
        in_seq = (col >= r0) & (col < r0 + SEQ_ROWS)
        last_col = r0 + SEQ_ROWS - 1
        for g in range(SSM_GROUPS):
            rows = slice(g * SSM_GROUP_DIM, (g + 1) * SSM_GROUP_DIM)
            h0 = st_ref[b, rows, :]
            c_g = c_rows[:, g * SSM_STATE:(g + 1) * SSM_STATE].astype(BF16)
            yoff_ref[r0:r0 + SEQ_ROWS, rows] = _dot_nt(c_g, h0.astype(BF16))
            lhs = jnp.where(in_seq, xw_t[rows, :], 0.0).astype(BF16)
            st = _dot(lhs, bm_b[:, g * SSM_STATE:(g + 1) * SSM_STATE])
            for j in range(SSM_HEADS_PER_GROUP):
                hh = SSM_HEADS_PER_GROUP * g + j
                hr = slice(j * SSM_HEAD_DIM, (j + 1) * SSM_HEAD_DIM)
                stn_ref[b, g * SSM_GROUP_DIM + j * SSM_HEAD_DIM:g * SSM_GROUP_DIM + (j + 1) * SSM_HEAD_DIM, :] = (
                    h0[hr, :] * total_t[hh:hh + 1, last_col:last_col + 1] + st[hr, :])

    y_ref[...] = (y_diag + yoff_ref[...] * _expand_heads(jnp.exp(acum), lane_g) + dskip_ref[...] * xs)


def _sample_ssd_chained_kernel(cext_ref, pext_ref, dtext_ref, st_ref, convw_ref, convb_ref, dtb_ref, alog_ref,
                              dskip_ref, poolw_ref, pscale_ref, prev_ref, *rest):
    del prev_ref
    _sample_ssd_kernel(cext_ref, pext_ref, dtext_ref, st_ref, convw_ref, convb_ref, dtb_ref, alog_ref,
                       dskip_ref, poolw_ref, pscale_ref, *rest)


def _sample_ssd_call(layer, cext, pext, dtext, state, new_state, p):
    n_seq = state.shape[1]
    R = SAMPLE_SEQ_TILE * SEQ_ROWS
    RP = SAMPLE_SEQ_TILE * POOL_SEQ_ROWS

    def vspec(rows, cols):
        return _const_spec((None, rows, cols), (layer, 0, 0))

    st_spec = pl.BlockSpec((None, SAMPLE_SEQ_TILE, SSM_INNER, SSM_STATE), lambda i: (layer, i, 0, 0))
    in_specs = [pl.BlockSpec((R, CONV_DIM), lambda i: (i, 0)),
                pl.BlockSpec((RP, POOL_DIM), lambda i: (i, 0)),
                pl.BlockSpec((R, LANES), lambda i: (i, 0)),
                st_spec,
                vspec(CONV_WIDTH, CONV_DIM), vspec(1, CONV_DIM), vspec(1, LANES), vspec(1, LANES),
                vspec(1, SSM_INNER),
                _const_spec((None, 4, POOL_GROUP_DIM, POOL_GROUP_DIM), (layer, 0, 0, 0)),
                vspec(1, POOL_DIM)]
    args = [cext, pext, dtext, state, p["conv_w"], p["conv_b"], p["dt_bias"], p["a_log"], p["d_skip"],
            p["pool_w"], p["pool_scale"]]
    kern, aliases = _sample_ssd_kernel, {}
    if new_state is not None:
        in_specs.append(pl.BlockSpec(memory_space=pl.ANY))
        args.append(new_state)
        kern, aliases = _sample_ssd_chained_kernel, {len(args) - 1: 2}
    return pl.pallas_call(
        kern,
        out_shape=(jax.ShapeDtypeStruct((n_seq * SEQ_ROWS, SSM_INNER), F32),
                   jax.ShapeDtypeStruct((n_seq * POOL_SEQ_ROWS, POOL_DIM), F32),
                   jax.ShapeDtypeStruct(state.shape, F32)),
        grid=(n_seq // SAMPLE_SEQ_TILE,),
        in_specs=in_specs,
        out_specs=(pl.BlockSpec((R, SSM_INNER), lambda i: (i, 0)),
                   pl.BlockSpec((RP, POOL_DIM), lambda i: (i, 0)),
                   st_spec),
        scratch_shapes=[pltpu.VMEM((R, SSM_INNER), F32)],
        input_output_aliases=aliases,
        compiler_params=pltpu.CompilerParams(dimension_semantics=("arbitrary",),
                                             vmem_limit_bytes=VMEM_LIMIT),
        name="sample_ssd",
    )(*args)


def _sample_attn_kernel(q_ref, k_ref, v_ref, o_ref, s_ref):
    n_mem = k_ref.shape[1] // MEM_HEADS
    HR = MEM_HEADS * SEQ_ROWS
    row_head = lax.broadcasted_iota(jnp.int32, (HR, MEM_DIM), 0) // SEQ_ROWS
    col_head = lax.broadcasted_iota(jnp.int32, (HR, MEM_DIM), 1) // MEM_HEAD_DIM
    own_head = row_head == col_head

    def heads_on_lanes(ref, b):
        return jnp.concatenate([ref[b, pl.ds(h, n_mem, stride=MEM_HEADS), :] for h in range(MEM_HEADS)],
                               axis=1).astype(BF16)

    for b in range(SAMPLE_ATTN_TILE):
        q = q_ref[b * SEQ_ROWS:(b + 1) * SEQ_ROWS, :]
        q_heads = jnp.where(own_head, jnp.concatenate([q] * MEM_HEADS, axis=0), 0.0).astype(BF16)
        s_ref[b * HR:(b + 1) * HR, :] = _dot_nt(q_heads, heads_on_lanes(k_ref, b))
    s = s_ref[...] * (MEM_HEAD_DIM ** -0.5)
    e = jnp.exp(s - jnp.max(s, axis=-1, keepdims=True))
    p = (e * (1.0 / jnp.sum(e, axis=-1, keepdims=True))).astype(BF16)
    for b in range(SAMPLE_ATTN_TILE):
        o = _dot(p[b * HR:(b + 1) * HR, :], heads_on_lanes(v_ref, b))
        o_ref[b * SEQ_ROWS:(b + 1) * SEQ_ROWS, :] = jnp.concatenate(
            [o[h * SEQ_ROWS:(h + 1) * SEQ_ROWS, h * MEM_HEAD_DIM:(h + 1) * MEM_HEAD_DIM]
             for h in range(MEM_HEADS)], axis=1)


def _sample_attn_call(layer, qext, cache_k, cache_v):
    n_seq, kv_rows = cache_k.shape[1], cache_k.shape[2]
    R = SAMPLE_ATTN_TILE * SEQ_ROWS
    kv_spec = pl.BlockSpec((None, SAMPLE_ATTN_TILE, kv_rows, MEM_HEAD_DIM), lambda i: (layer, i, 0, 0))
    return pl.pallas_call(
        _sample_attn_kernel,
        out_shape=jax.ShapeDtypeStruct((n_seq * SEQ_ROWS, MEM_DIM), F32),
        grid=(n_seq // SAMPLE_ATTN_TILE,),
        in_specs=[pl.BlockSpec((R, MEM_DIM), lambda i: (i, 0)), kv_spec, kv_spec],
        out_specs=pl.BlockSpec((R, MEM_DIM), lambda i: (i, 0)),
        scratch_shapes=[pltpu.VMEM((SAMPLE_ATTN_TILE * MEM_HEADS * SEQ_ROWS, kv_rows // MEM_HEADS), F32)],
        compiler_params=pltpu.CompilerParams(dimension_semantics=("arbitrary",),
                                             vmem_limit_bytes=VMEM_LIMIT),
        name="sample_attn",
    )(qext, cache_k, cache_v)


def _sample_merge_kernel(x_ref, z_ref, g_ref, y_ref, ypool_ref, ymem_ref, gbias_ref, snorm_ref,
                         wbp_ref, wbs_ref, wbm_ref, wo_ref, xo_ref):
    xo_ref[...] = _merge_tail(x_ref[...], _silu(z_ref[...]), jax.nn.sigmoid(g_ref[...] + gbias_ref[...]),
                              y_ref[...], ypool_ref[...], ymem_ref[...], snorm_ref[...],
                              wbp_ref, wbs_ref, wbm_ref, wo_ref)


def _sample_merge_call(x, layer, row_block, n_rows, z, graw, y_raw, y_pool, y_mem, p):
    def full(cols):
        return pl.BlockSpec((n_rows, cols), lambda i: (0, 0))

    def wspec(rows, cols):
        return _const_spec((None, rows, cols), (layer, 0, 0))

    x_spec = pl.BlockSpec((n_rows, D_MODEL), lambda i: (row_block, 0))
    return pl.pallas_call(
        _sample_merge_kernel,
        out_shape=jax.ShapeDtypeStruct(x.shape, F32),
        grid=(1,),
        in_specs=[x_spec, full(SSM_INNER), full(N_BRANCH * D_MODEL), full(SSM_INNER), full(POOL_DIM),
                  full(MEM_DIM), wspec(1, N_BRANCH * D_MODEL), wspec(1, SSM_INNER),
                  wspec(POOL_DIM, _pitch(D_MODEL)), wspec(SSM_INNER, _pitch(D_MODEL)),
                  wspec(MEM_DIM, _pitch(D_MODEL)), wspec(D_MODEL, _pitch(D_MODEL))],
        out_specs=x_spec,
        input_output_aliases={0: 0},
        compiler_params=pltpu.CompilerParams(dimension_semantics=("arbitrary",),
                                             vmem_limit_bytes=VMEM_LIMIT),
        name="sample_merge",
    )(x, z, graw, y_raw, y_pool, y_mem, p["gate_bias"], p["ssm_norm"],
      p["w_br_pool"], p["w_br_ssm"], p["w_br_mem"], p["w_o"])


def _row3(a):
    return a.reshape(a.shape[0], 1, a.shape[1])


def _pad_lanes(a):
    return jnp.pad(a, [(0, 0)] * (a.ndim - 1) + [(0, LANES - a.shape[-1])])


def _token_rows(ext, rows_per_seq, n_tok):
    n_seq = ext.shape[0] // rows_per_seq
    return ext.reshape(n_seq, rows_per_seq, -1)[:, rows_per_seq - n_tok:, :].reshape(n_seq * n_tok, -1)


def kernel(x_prompt, x_sample, mem_prompt, state_pool, state_conv, state_ssm, cache_mem_k, cache_mem_v,
           ffn1_norm, ffn1_w_gate, ffn1_w_up, ffn1_w_down, mix_norm, w_in, gate_bias,
           pool_w, pool_scale, conv_w, conv_b, dt_bias, a_log, d_skip, ssm_norm,
           mem_norm, w_mem_k, w_mem_v, w_br_pool, w_br_ssm, w_br_mem, w_o,
           ffn2_norm, ffn2_w_gate, ffn2_w_up, ffn2_w_down, final_norm):
    depth = w_in.shape[0]
    batch, seq, _ = x_prompt.shape
    n_seq, n_tok, _ = x_sample.shape
    n_mem = mem_prompt.shape[1]
    n_prompt = batch * seq
    n_sample = n_seq * n_tok
    assert n_sample == FFN_TOKEN_TILE and n_prompt % FFN_TOKEN_TILE == 0 and seq % PROMPT_CHUNK == 0
    assert n_tok == SEQ_ROWS // 2 and PAST_LEN >= max(POOL_WINDOWS)
    assert n_seq % SAMPLE_SEQ_TILE == 0 and n_seq % SAMPLE_ATTN_TILE == 0
    sample_block = n_prompt // n_sample

    w_pool, w_z, w_xbc, w_dt, w_q, w_g = _split_w_in_call(w_in)
    params = dict(
        mix_norm=_row3(mix_norm),
        w_pool=w_pool, w_z=w_z, w_xbc=w_xbc, w_dt=w_dt, w_q=w_q, w_g=w_g,
        gate_bias=_row3(gate_bias), pool_w=pool_w.astype(BF16), pool_scale=_row3(pool_scale),
        conv_w=conv_w, conv_b=_row3(conv_b), dt_bias=_row3(_pad_lanes(dt_bias)), a_log=_row3(_pad_lanes(a_log)),
        d_skip=_row3(jnp.repeat(d_skip, SSM_HEAD_DIM, axis=1)), ssm_norm=_row3(ssm_norm),
        w_br_pool=_pad_pitch(w_br_pool.astype(BF16)), w_br_ssm=_pad_pitch(w_br_ssm.astype(BF16)),
        w_br_mem=_pad_pitch(w_br_mem.astype(BF16)), w_o=_pad_pitch(w_o.astype(BF16)))
    ffn1 = (_row3(ffn1_norm), ffn1_w_gate.astype(BF16), ffn1_w_up.astype(BF16),
            _pad_pitch(ffn1_w_down.astype(BF16)))
    ffn2 = (_row3(ffn2_norm), ffn2_w_gate.astype(BF16), ffn2_w_up.astype(BF16),
            _pad_pitch(ffn2_w_down.astype(BF16)))

    mem_k, mem_v, mem_kb, mem_vb = _memkv_call(mem_prompt, _row3(mem_norm), w_mem_k.astype(BF16),
                                               w_mem_v.astype(BF16))
    cache_k = cache_mem_k.reshape(depth, n_seq, n_mem * MEM_HEADS, MEM_HEAD_DIM)
    cache_v = cache_mem_v.reshape(depth, n_seq, n_mem * MEM_HEADS, MEM_HEAD_DIM)
    state_ssm_rows = state_ssm.reshape(depth, n_seq, SSM_INNER, SSM_STATE)

    pool_p, conv_p, ssm_p, pool_s, conv_s = [], [], [], [], []
    ssm_s = None
    for l in range(depth):
        if l == 0:
            x = _ffn_first_call(x_prompt.reshape(n_prompt, D_MODEL), x_sample.reshape(n_sample, D_MODEL), l, ffn1)
        else:
            x = _ffn_call(x, l, ffn1)
        x, npool, nconv, nssm = _prompt_mixer_call(x, l, batch, seq, params, mem_kb, mem_vb)
        pool_p.append(npool)
        conv_p.append(nconv)
        ssm_p.append(nssm.reshape(batch, SSM_HEADS, SSM_HEAD_DIM, SSM_STATE))

        u, z, xbc, dt_raw, q, graw = _sample_inproj_call(x, l, sample_block, n_sample, params)
        u3 = u.reshape(n_seq, n_tok, POOL_DIM)
        xbc3 = xbc.reshape(n_seq, n_tok, CONV_DIM)
        pext = jnp.concatenate([jnp.zeros((n_seq, POOL_SEQ_ROWS - POOL_BUF - n_tok, POOL_DIM), F32),
                                state_pool[l], u3], axis=1).reshape(n_seq * POOL_SEQ_ROWS, POOL_DIM)
        cext = jnp.concatenate([jnp.zeros((n_seq, SEQ_ROWS - CONV_WIDTH + 1 - n_tok, CONV_DIM), F32),
                                state_conv[l], xbc3], axis=1).reshape(n_seq * SEQ_ROWS, CONV_DIM)

        def pad_seq(a):
            a3 = a.reshape(n_seq, n_tok, a.shape[-1])
            return jnp.concatenate([jnp.zeros_like(a3), a3], axis=1).reshape(n_seq * SEQ_ROWS, a.shape[-1])

        y_ext, ypool_ext, ssm_s = _sample_ssd_call(l, cext, pext, pad_seq(dt_raw), state_ssm_rows, ssm_s, params)
        ymem_ext = _sample_attn_call(l, pad_seq(q), cache_k, cache_v)
        x = _sample_merge_call(x, l, sample_block, n_sample, z, graw,
                               _token_rows(y_ext, SEQ_ROWS, n_tok), _token_rows(ypool_ext, POOL_SEQ_ROWS, n_tok),
                               _token_rows(ymem_ext, SEQ_ROWS, n_tok), params)
        pool_s.append(jnp.concatenate([state_pool[l][:, n_tok:], u3], axis=1))
        conv_s.append(xbc3[:, n_tok - (CONV_WIDTH - 1):])

        if l < depth - 1:
            x = _ffn_call(x, l, ffn2)
    y_prompt, y_sample = _ffn_final_call(x, n_prompt, depth - 1, ffn2, final_norm.reshape(1, D_MODEL))

    mem_shape = (depth, batch, n_mem, MEM_HEADS, MEM_HEAD_DIM)
    return (y_prompt.reshape(batch, seq, D_MODEL), y_sample.reshape(n_seq, n_tok, D_MODEL),
            jnp.stack(pool_p), jnp.stack(conv_p), jnp.stack(ssm_p),
            mem_k.reshape(mem_shape), mem_v.reshape(mem_shape),
            jnp.stack(pool_s), jnp.stack(conv_s),
            ssm_s.reshape(depth, n_seq, SSM_HEADS, SSM_HEAD_DIM, SSM_STATE))
```

```python
import functools

import jax
import jax.numpy as jnp
from jax import lax
from jax.experimental import pallas as pl
from jax.experimental.pallas import tpu as pltpu

F32 = jnp.float32
BF16 = jnp.bfloat16

D_MODEL = 1024
D_FF = 2816
EPS = 1e-6
PAST_LEN = 16384

POOL_WINDOWS = (2, 4, 8, 16)
POOL_GROUP_DIM = 128
POOL_DIM = 512
POOL_BUF = 15
SSM_INNER = 1024
SSM_HEADS = 16
SSM_HEAD_DIM = 64
SSM_GROUPS = 4
SSM_HEADS_PER_GROUP = 4
SSM_GROUP_DIM = 256
SSM_STATE = 128
CONV_WIDTH = 4
CONV_DIM = 2048
MEM_HEADS = 4
MEM_HEAD_DIM = 128
MEM_DIM = 512
N_BRANCH = 3
IN_SPLITS = (POOL_DIM, SSM_INNER, CONV_DIM, SSM_HEADS, MEM_DIM, N_BRANCH * D_MODEL)

SUBLANES = 8
LANES = 128

FFN_TOKEN_TILE = 512
FFN_FF_CHUNK = 2816
PROMPT_CHUNK = 256
SAMPLE_SEQ_TILE = 16
SAMPLE_ATTN_TILE = 8
MEMKV_BATCH_TILE = 4
W_IN_ROW_TILE = 256
SEQ_ROWS = 8
POOL_SEQ_ROWS = 24
VMEM_LIMIT = 56 * 1024 * 1024

_NT = (((1,), (1,)), ((), ()))
_TN = (((0,), (0,)), ((), ()))


def _dot(a, b):
    return jnp.dot(a, b, preferred_element_type=F32)


def _dot_nt(a, b):
    return lax.dot_general(a, b, _NT, preferred_element_type=F32)


def _rmsnorm(x, g):
    return x * lax.rsqrt(jnp.mean(x * x, axis=-1, keepdims=True) + EPS) * g


def _silu(x):
    return x * jax.nn.sigmoid(x)


def _softplus(x):
    return jnp.maximum(x, 0.0) + jnp.log1p(jnp.exp(-jnp.abs(x)))


def _pitch(cols):
    return cols + LANES if cols % (SUBLANES * LANES) == 0 else cols


def _const_spec(shape, index, single_buffer=True):
    mode = pl.Buffered(1) if single_buffer else None
    return pl.BlockSpec(shape, lambda *_: index, pipeline_mode=mode)


def _expand_group(v, g, lane):
    h0 = SSM_HEADS_PER_GROUP * g
    out = jnp.broadcast_to(v[:, h0 + 3:h0 + 4], lane.shape)
    for j in (2, 1, 0):
        out = jnp.where(lane < SSM_HEAD_DIM * (j + 1), v[:, h0 + j:h0 + j + 1], out)
    return out


def _expand_heads(v, lane):
    return jnp.concatenate([_expand_group(v, g, lane) for g in range(SSM_GROUPS)], axis=1)


def _ffn_body(x, g, wg_ref, wu_ref, wd_ref):
    hn = _rmsnorm(x, g).astype(BF16)
    acc = None
    for j in range(D_FF // FFN_FF_CHUNK):
        sl = slice(j * FFN_FF_CHUNK, (j + 1) * FFN_FF_CHUNK)
        gate = _dot(hn, wg_ref[:, sl])
        up = _dot(hn, wu_ref[:, sl])
        part = _dot((_silu(gate) * up).astype(BF16), wd_ref[sl, :])
        acc = part if acc is None else acc + part
    return x + 0.5 * acc


def _ffn_kernel(x_ref, g_ref, wg_ref, wu_ref, wd_ref, o_ref):
    o_ref[...] = _ffn_body(x_ref[...], g_ref[...], wg_ref, wu_ref, wd_ref)


def _ffn_first_kernel(xp_ref, xs_ref, g_ref, wg_ref, wu_ref, wd_ref, o_ref, *, prompt_blocks):
    x = jnp.where(pl.program_id(0) < prompt_blocks, xp_ref[...], xs_ref[...])
    o_ref[...] = _ffn_body(x, g_ref[...], wg_ref, wu_ref, wd_ref)


def _ffn_final_kernel(x_ref, g_ref, wg_ref, wu_ref, wd_ref, fg_ref, yp_ref, ys_ref, *, prompt_blocks):
    y = _rmsnorm(_ffn_body(x_ref[...], g_ref[...], wg_ref, wu_ref, wd_ref), fg_ref[...])
    i = pl.program_id(0)

    @pl.when(i < prompt_blocks)
    def _():
        yp_ref[...] = y

    @pl.when(i == prompt_blocks)
    def _():
        ys_ref[...] = y


def _ffn_weight_specs(layer):
    return [_const_spec((None, 1, D_MODEL), (layer, 0, 0)),
            _const_spec((None, D_MODEL, D_FF), (layer, 0, 0)),
            _const_spec((None, D_MODEL, D_FF), (layer, 0, 0)),
            _const_spec((None, D_FF, D_MODEL), (layer, 0, 0))]


_FFN_PARAMS = pltpu.CompilerParams(dimension_semantics=("arbitrary",), vmem_limit_bytes=VMEM_LIMIT)


def _ffn_call(x, layer, weights):
    tile = pl.BlockSpec((FFN_TOKEN_TILE, D_MODEL), lambda i: (i, 0))
    return pl.pallas_call(
        _ffn_kernel,
        out_shape=jax.ShapeDtypeStruct(x.shape, F32),
        grid=(x.shape[0] // FFN_TOKEN_TILE,),
        in_specs=[tile] + _ffn_weight_specs(layer),
        out_specs=tile,
        input_output_aliases={0: 0},
        compiler_params=_FFN_PARAMS,
        name="ffn",
    )(x, *weights)


def _ffn_first_call(x_prompt, x_sample, layer, weights):
    prompt_blocks = x_prompt.shape[0] // FFN_TOKEN_TILE
    n_tok = x_prompt.shape[0] + x_sample.shape[0]
    tile = pl.BlockSpec((FFN_TOKEN_TILE, D_MODEL), lambda i: (i, 0))
    p_tile = pl.BlockSpec((FFN_TOKEN_TILE, D_MODEL), lambda i: (jnp.minimum(i, prompt_blocks - 1), 0))
    s_tile = pl.BlockSpec((FFN_TOKEN_TILE, D_MODEL), lambda i: (0, 0))
    return pl.pallas_call(
        functools.partial(_ffn_first_kernel, prompt_blocks=prompt_blocks),
        out_shape=jax.ShapeDtypeStruct((n_tok, D_MODEL), F32),
        grid=(n_tok // FFN_TOKEN_TILE,),
        in_specs=[p_tile, s_tile] + _ffn_weight_specs(layer),
        out_specs=tile,
        compiler_params=_FFN_PARAMS,
        name="ffn_first",
    )(x_prompt, x_sample, *weights)


def _ffn_final_call(x, n_prompt, layer, weights, final_norm):
    prompt_blocks = n_prompt // FFN_TOKEN_TILE
    n_tok = x.shape[0]
    tile = pl.BlockSpec((FFN_TOKEN_TILE, D_MODEL), lambda i: (i, 0))
    p_tile = pl.BlockSpec((FFN_TOKEN_TILE, D_MODEL), lambda i: (jnp.minimum(i, prompt_blocks - 1), 0))
    s_tile = pl.BlockSpec((FFN_TOKEN_TILE, D_MODEL), lambda i: (0, 0))
    return pl.pallas_call(
        functools.partial(_ffn_final_kernel, prompt_blocks=prompt_blocks),
        out_shape=(jax.ShapeDtypeStruct((n_prompt, D_MODEL), F32),
                   jax.ShapeDtypeStruct((n_tok - n_prompt, D_MODEL), F32)),
        grid=(n_tok // FFN_TOKEN_TILE,),
        in_specs=[tile] + _ffn_weight_specs(layer) + [_const_spec((1, D_MODEL), (0, 0))],
        out_specs=(p_tile, s_tile),
        compiler_params=_FFN_PARAMS,
        name="ffn_final",
    )(x, *weights, final_norm)


def _split_w_in_kernel(wt_ref, *out_refs):
    start = 0
    for o_ref, cols in zip(out_refs, IN_SPLITS):
        stored = o_ref.shape[-1]
        if cols >= LANES:
            o_ref[:, :cols] = wt_ref[start:start + cols, :].T.astype(BF16)
            if stored > cols:
                o_ref[:, cols:] = jnp.zeros((o_ref.shape[0], stored - cols), BF16)
        else:
            tile = wt_ref[start:start + stored, :].T
            lane = lax.broadcasted_iota(jnp.int32, tile.shape, 1)
            o_ref[...] = jnp.where(lane < cols, tile, 0.0).astype(BF16)
        start += cols


def _split_w_in_call(w_in):
    depth, rows, total = w_in.shape
    stored = [_pitch(c) if c >= LANES else LANES for c in IN_SPLITS]
    tile = W_IN_ROW_TILE
    return pl.pallas_call(
        _split_w_in_kernel,
        out_shape=tuple(jax.ShapeDtypeStruct((depth, rows, s), BF16) for s in stored),
        grid=(depth, rows // tile),
        in_specs=[pl.BlockSpec((None, total, tile), lambda l, r: (l, 0, r))],
        out_specs=tuple(pl.BlockSpec((None, tile, s), lambda l, r: (l, r, 0)) for s in stored),
        compiler_params=pltpu.CompilerParams(dimension_semantics=("arbitrary", "arbitrary"),
                                             vmem_limit_bytes=VMEM_LIMIT),
        name="split_w_in",
    )(jnp.swapaxes(w_in, 1, 2))


def _memkv_kernel(m_ref, g_ref, wk_ref, wv_ref, k_ref, v_ref, kb_ref, vb_ref):
    nb, n_mem, _ = m_ref.shape
    hn = _rmsnorm(m_ref[...].reshape(nb * n_mem, D_MODEL), g_ref[...]).astype(BF16)
    k = _dot(hn, wk_ref[...])
    v = _dot(hn, wv_ref[...])
    for b in range(nb):
        rows = slice(b * n_mem, (b + 1) * n_mem)
        for h in range(MEM_HEADS):
            hs = slice(h * MEM_HEAD_DIM, (h + 1) * MEM_HEAD_DIM)
            k_ref[b, pl.ds(h, n_mem, stride=MEM_HEADS), :] = k[rows, hs]
            v_ref[b, pl.ds(h, n_mem, stride=MEM_HEADS), :] = v[rows, hs]
        kb_ref[b] = k[rows, :].astype(BF16)
        vb_ref[b] = v[rows, :].astype(BF16)


def _memkv_call(mem, norm, wk, wv):
    depth = norm.shape[0]
    batch, n_mem, _ = mem.shape
    nb = MEMKV_BATCH_TILE
    f32_spec = pl.BlockSpec((None, nb, n_mem * MEM_HEADS, MEM_HEAD_DIM), lambda l, b: (l, b, 0, 0))
    out_spec = pl.BlockSpec((None, nb, n_mem, MEM_DIM), lambda l, b: (l, b, 0, 0))
    w_spec = pl.BlockSpec((None, D_MODEL, MEM_DIM), lambda l, b: (l, 0, 0))
    f32_out = jax.ShapeDtypeStruct((depth, batch, n_mem * MEM_HEADS, MEM_HEAD_DIM), F32)
    bf16_out = jax.ShapeDtypeStruct((depth, batch, n_mem, MEM_DIM), BF16)
    return pl.pallas_call(
        _memkv_kernel,
        out_shape=(f32_out, f32_out, bf16_out, bf16_out),
        grid=(depth, batch // nb),
        in_specs=[pl.BlockSpec((nb, n_mem, D_MODEL), lambda l, b: (b, 0, 0)),
                  pl.BlockSpec((None, 1, D_MODEL), lambda l, b: (l, 0, 0)),
                  w_spec, w_spec],
        out_specs=(f32_spec, f32_spec, out_spec, out_spec),
        compiler_params=pltpu.CompilerParams(dimension_semantics=("arbitrary", "arbitrary"),
                                             vmem_limit_bytes=VMEM_LIMIT),
        name="mem_kv",
    )(mem, norm, wk, wv)


def _merge_tail(x, silu_z, gates, y_raw, y_pool, y_mem, snorm, wbp_ref, wbs_ref, wbm_ref, wo_ref):
    yz = y_raw * silu_z
    parts = []
    for g in range(SSM_GROUPS):
        yg = yz[:, g * SSM_GROUP_DIM:(g + 1) * SSM_GROUP_DIM]
        parts.append(yg * lax.rsqrt(jnp.mean(yg * yg, axis=-1, keepdims=True) + EPS))
    y_ssm = (jnp.concatenate(parts, axis=1) * snorm).astype(BF16)
    merged = (gates[:, :D_MODEL] * _dot(y_pool.astype(BF16), wbp_ref[:, :D_MODEL])
              + gates[:, D_MODEL:2 * D_MODEL] * _dot(y_ssm, wbs_ref[:, :D_MODEL])
              + gates[:, 2 * D_MODEL:] * _dot(y_mem.astype(BF16), wbm_ref[:, :D_MODEL]))
    return x + _dot(merged.astype(BF16), wo_ref[:, :D_MODEL])


def _attend(q, k, v):
    s = _dot_nt(q, k) * (MEM_HEAD_DIM ** -0.5)
    e = jnp.exp(s - jnp.max(s, axis=-1, keepdims=True))
    p = e * (1.0 / jnp.sum(e, axis=-1, keepdims=True))
    return _dot(p.astype(BF16), v)


def _prompt_mixer_kernel(x_ref, norm_ref, wpool_ref, wz_ref, wxbc_ref, wdt_ref, wq_ref, wg_ref,
                         gbias_ref, poolw_ref, pscale_ref, convw_ref, convb_ref, dtb_ref, alog_ref,
                         dskip_ref, snorm_ref, wbp_ref, wbs_ref, wbm_ref, wo_ref, k_ref, v_ref,
                         xo_ref, poolst_ref, convst_ref, h_ref,
                         cbuf_ref, pbuf_ref):
    T = PROMPT_CHUNK
    W = SSM_GROUP_DIM
    c = pl.program_id(1)

    @pl.when(c == 0)
    def _():
        h_ref[...] = jnp.zeros_like(h_ref)
        cbuf_ref[0:SUBLANES, :] = jnp.zeros((SUBLANES, CONV_DIM), F32)
        pbuf_ref[0:2 * SUBLANES, :] = jnp.zeros((2 * SUBLANES, POOL_DIM), F32)

    x = x_ref[...]
    hn = _rmsnorm(x, norm_ref[...]).astype(BF16)

    dt = _softplus(_dot(hn, wdt_ref[...]) + dtb_ref[...])
    la = dt * (-jnp.exp(alog_ref[...]))
    la_t = la.T[0:SSM_HEADS, :]
    lane_t = lax.broadcasted_iota(jnp.int32, (SSM_HEADS, T), 1)
    acum_t = la_t
    shift = 1
    while shift < T:
        acum_t = acum_t + jnp.where(lane_t >= shift, pltpu.roll(acum_t, shift, 1), 0.0)
        shift *= 2
    acum = jnp.concatenate([acum_t, jnp.zeros((LANES - SSM_HEADS, T), F32)], axis=0).T
    from_start = jnp.exp(acum)
    to_end = jnp.exp(acum[T - 1:T, :] - acum)
    chunk_decay = jnp.exp(acum_t[:, T - 1:T])

    P0 = 2 * SUBLANES
    u = _dot(hn, wpool_ref[...])
    pbuf_ref[P0:P0 + T, :] = u
    pos = c * T + lax.broadcasted_iota(jnp.int32, (T, 1), 0)

    C0 = SUBLANES
    n_blocks = CONV_DIM // W
    n_x = SSM_INNER // W
    xc = [None] * n_blocks
    wide_out = [None] * n_blocks
    y_pool = [None] * len(POOL_WINDOWS)
    q = _dot(hn, wq_ref[...])
    lane_g = lax.broadcasted_iota(jnp.int32, (T, W), 1)
    causal = (lax.broadcasted_iota(jnp.int32, (T, T), 0) >= lax.broadcasted_iota(jnp.int32, (T, T), 1))

    def wide_block(i):
        lo, hi = 2 * i * W, 2 * (i + 1) * W
        if hi <= N_BRANCH * D_MODEL:
            wide_out[i] = jax.nn.sigmoid(_dot(hn, wg_ref[:, lo:hi]) + gbias_ref[:, lo:hi])
        else:
            lo, hi = lo - N_BRANCH * D_MODEL, hi - N_BRANCH * D_MODEL
            wide_out[i] = _silu(_dot(hn, wz_ref[:, lo:hi]))

    def conv_block(j):
        cols = slice(j * W, (j + 1) * W)
        xbc = _dot(hn, wxbc_ref[:, cols])
        cbuf_ref[C0:C0 + T, cols] = xbc
        acc = convb_ref[:, cols] + convw_ref[CONV_WIDTH - 1:CONV_WIDTH, cols] * xbc
        for k in range(CONV_WIDTH - 1):
            off = C0 - (CONV_WIDTH - 1) + k
            acc = acc + convw_ref[k:k + 1, cols] * cbuf_ref[off:off + T, cols]
        xc[j] = _silu(acc)
        convst_ref[:, cols] = xbc[T - (CONV_WIDTH - 1):, :]
        cbuf_ref[0:C0, cols] = xbc[T - C0:, :]

    def pool_group(j):
        w = POOL_WINDOWS[j]
        cs = slice(j * POOL_GROUP_DIM, (j + 1) * POOL_GROUP_DIM)
        ug = u[:, cs]
        s = ug
        for k in range(1, w):
            s = s + pbuf_ref[P0 - k:P0 - k + T, cs]
        inv_cnt = 1.0 / jnp.minimum(pos + 1, w).astype(F32)
        y_pool[j] = _dot((s * inv_cnt - ug).astype(BF16), poolw_ref[j])

    def ssd_group(g):
        half = slice((g % 2) * SSM_STATE, (g % 2 + 1) * SSM_STATE)
        xs_g = xc[g]
        bm_g = xc[n_x + g // 2][:, half].astype(BF16)
        cm_g = xc[n_x + SSM_GROUPS // 2 + g // 2][:, half].astype(BF16)
        cb = _dot_nt(cm_g, bm_g)
        xdt = xs_g * _expand_group(dt, g, lane_g)
        xdt_b = xdt.astype(BF16)
        lhs, rhs = [], []
        for j in range(SSM_HEADS_PER_GROUP):
            hh = SSM_HEADS_PER_GROUP * g + j
            seg = acum[:, hh:hh + 1] - acum_t[hh:hh + 1, :]
            decay = jnp.exp(jnp.where(causal, seg, -jnp.inf))
            lhs.append((cb * decay).astype(BF16))
            in_head = (lane_g >= SSM_HEAD_DIM * j) & (lane_g < SSM_HEAD_DIM * (j + 1))
            rhs.append(jnp.where(in_head, xdt_b, jnp.zeros_like(xdt_b)))
        y_diag = _dot(jnp.concatenate(lhs, axis=1), jnp.concatenate(rhs, axis=0))
        rows = slice(g * W, (g + 1) * W)
        h_prev = h_ref[rows, :]
        y_off = _dot_nt(cm_g, h_prev.astype(BF16)) * _expand_group(from_start, g, lane_g)
        xw = (xdt * _expand_group(to_end, g, lane_g)).astype(BF16)
        st = lax.dot_general(xw, bm_g, _TN, preferred_element_type=F32)
        for j in range(SSM_HEADS_PER_GROUP):
            hh = SSM_HEADS_PER_GROUP * g + j
            hr = slice(j * SSM_HEAD_DIM, (j + 1) * SSM_HEAD_DIM)
            h_ref[g * W + j * SSM_HEAD_DIM:g * W + (j + 1) * SSM_HEAD_DIM, :] = (
                h_prev[hr, :] * chunk_decay[hh:hh + 1, :] + st[hr, :])
        return y_diag + y_off + dskip_ref[:, rows] * xs_g

    def attend_head(h):
        hs = slice(h * MEM_HEAD_DIM, (h + 1) * MEM_HEAD_DIM)
        return _attend(q[:, hs].astype(BF16), k_ref[:, hs], v_ref[:, hs])

    for j in range(n_blocks):
        wide_block(j)
        conv_block(j)
        if j < len(POOL_WINDOWS):
            pool_group(j)
    y_groups, y_mem = [], []
    for g in range(SSM_GROUPS):
        y_groups.append(ssd_group(g))
        y_mem.append(attend_head(g))
    y_pool = jnp.concatenate(y_pool, axis=1) * pscale_ref[...]
    poolst_ref[...] = u[T - POOL_BUF:, :]
    pbuf_ref[0:P0, :] = u[T - P0:, :]
    n_gate = N_BRANCH * D_MODEL // (2 * W)
    gates = jnp.concatenate(wide_out[:n_gate], axis=1)
    silu_z = jnp.concatenate(wide_out[n_gate:], axis=1)

    xo_ref[...] = _merge_tail(x, silu_z, gates,
                              jnp.concatenate(y_groups, axis=1), y_pool, jnp.concatenate(y_mem, axis=1),
                              snorm_ref[...], wbp_ref, wbs_ref, wbm_ref, wo_ref)


def _prompt_mixer_call(x, layer, batch, seq, p, kb, vb):
    T = PROMPT_CHUNK
    n_chunks = seq // T
    n_mem = kb.shape[2]
    x_spec = pl.BlockSpec((T, D_MODEL), lambda b, c: (b * n_chunks + c, 0))

    def wspec(rows, cols):
        return _const_spec((None, rows, cols), (layer, 0, 0))

    def vspec(cols):
        return _const_spec((None, 1, cols), (layer, 0, 0))

    kv_spec = pl.BlockSpec((None, None, n_mem, MEM_DIM), lambda b, c: (layer, b, 0, 0))
    in_specs = [x_spec, vspec(D_MODEL),
                wspec(D_MODEL, POOL_DIM), wspec(D_MODEL, _pitch(SSM_INNER)), wspec(D_MODEL, _pitch(CONV_DIM)),
                wspec(D_MODEL, LANES), wspec(D_MODEL, MEM_DIM), wspec(D_MODEL, _pitch(N_BRANCH * D_MODEL)),
                vspec(N_BRANCH * D_MODEL),
                _const_spec((None, 4, POOL_GROUP_DIM, POOL_GROUP_DIM), (layer, 0, 0, 0)),
                vspec(POOL_DIM), wspec(CONV_WIDTH, CONV_DIM), vspec(CONV_DIM), vspec(LANES), vspec(LANES),
                vspec(SSM_INNER), vspec(SSM_INNER),
                wspec(POOL_DIM, D_MODEL), wspec(SSM_INNER, D_MODEL), wspec(MEM_DIM, D_MODEL),
                wspec(D_MODEL, D_MODEL), kv_spec, kv_spec]
    out_shape = (jax.ShapeDtypeStruct(x.shape, F32),
                 jax.ShapeDtypeStruct((batch, POOL_BUF, POOL_DIM), F32),
                 jax.ShapeDtypeStruct((batch, CONV_WIDTH - 1, CONV_DIM), F32),
                 jax.ShapeDtypeStruct((batch, SSM_INNER, SSM_STATE), F32))
    out_specs = (x_spec,
                 pl.BlockSpec((None, POOL_BUF, POOL_DIM), lambda b, c: (b, 0, 0)),
                 pl.BlockSpec((None, CONV_WIDTH - 1, CONV_DIM), lambda b, c: (b, 0, 0)),
                 pl.BlockSpec((None, SSM_INNER, SSM_STATE), lambda b, c: (b, 0, 0)))
    return pl.pallas_call(
        _prompt_mixer_kernel,
        out_shape=out_shape,
        grid=(batch, n_chunks),
        in_specs=in_specs,
        out_specs=out_specs,
        scratch_shapes=[pltpu.VMEM((SUBLANES + T, CONV_DIM), F32),
                        pltpu.VMEM((2 * SUBLANES + T, POOL_DIM), F32)],
        input_output_aliases={0: 0},
        compiler_params=pltpu.CompilerParams(dimension_semantics=("arbitrary", "arbitrary"),
                                             vmem_limit_bytes=VMEM_LIMIT),
        name="prompt_mixer",
    )(x, p["mix_norm"], p["w_pool"], p["w_z"], p["w_xbc"], p["w_dt"], p["w_q"], p["w_g"],
      p["gate_bias"], p["pool_w"], p["pool_scale"], p["conv_w"], p["conv_b"], p["dt_bias"], p["a_log"],
      p["d_skip"], p["ssm_norm"], p["w_br_pool"], p["w_br_ssm"], p["w_br_mem"], p["w_o"], kb, vb)


def _sample_inproj_kernel(x_ref, norm_ref, wpool_ref, wz_ref, wxbc_ref, wdt_ref, wq_ref, wg_ref,
                          u_ref, z_ref, xbc_ref, dt_ref, q_ref, g_ref):
    hn = _rmsnorm(x_ref[...], norm_ref[...]).astype(BF16)
    u_ref[...] = _dot(hn, wpool_ref[...])
    z_ref[...] = _dot(hn, wz_ref[:, :SSM_INNER])
    xbc_ref[...] = _dot(hn, wxbc_ref[:, :CONV_DIM])
    dt_ref[...] = _dot(hn, wdt_ref[...])
    q_ref[...] = _dot(hn, wq_ref[...])
    g_ref[...] = _dot(hn, wg_ref[:, :N_BRANCH * D_MODEL])


def _sample_inproj_call(x, layer, row_block, n_rows, p):
    def wspec(cols):
        return _const_spec((None, D_MODEL, cols), (layer, 0, 0))

    widths = (POOL_DIM, SSM_INNER, CONV_DIM, LANES, MEM_DIM, N_BRANCH * D_MODEL)
    return pl.pallas_call(
        _sample_inproj_kernel,
        out_shape=tuple(jax.ShapeDtypeStruct((n_rows, w), F32) for w in widths),
        grid=(1,),
        in_specs=[pl.BlockSpec((n_rows, D_MODEL), lambda i: (row_block, 0)),
                  _const_spec((None, 1, D_MODEL), (layer, 0, 0)),
                  wspec(POOL_DIM), wspec(_pitch(SSM_INNER)), wspec(_pitch(CONV_DIM)), wspec(LANES), wspec(MEM_DIM),
                  wspec(_pitch(N_BRANCH * D_MODEL))],
        out_specs=tuple(pl.BlockSpec((n_rows, w), lambda i: (0, 0)) for w in widths),
        compiler_params=pltpu.CompilerParams(dimension_semantics=("arbitrary",),
                                             vmem_limit_bytes=VMEM_LIMIT),
        name="sample_inproj",
    )(x, p["mix_norm"], p["w_pool"], p["w_z"], p["w_xbc"], p["w_dt"], p["w_q"], p["w_g"])


def _sample_ssd_kernel(cext_ref, pext_ref, dtext_ref, st_ref, convw_ref, convb_ref, dtb_ref, alog_ref,
                       dskip_ref, poolw_ref, pscale_ref,
                       y_ref, ypool_ref, stn_ref, yoff_ref):
    R = SAMPLE_SEQ_TILE * SEQ_ROWS
    n_tok = SEQ_ROWS // 2
    row = lax.broadcasted_iota(jnp.int32, (R, 1), 0)
    t_idx = row % SEQ_ROWS - n_tok
    tok = t_idx >= 0

    f = pext_ref[...]
    sums = []
    s = f
    for step in (1, 2, 4, 8):
        s = s + pltpu.roll(s, step, 0)
        sums.append(s)
    y_parts = []
    for g, w in enumerate(POOL_WINDOWS):
        cs = slice(g * POOL_GROUP_DIM, (g + 1) * POOL_GROUP_DIM)
        d = (sums[g][:, cs] * (1.0 / w) - f[:, cs]).astype(BF16)
        y_parts.append(_dot(d, poolw_ref[g]))
    ypool_ref[...] = jnp.concatenate(y_parts, axis=1) * pscale_ref[...]

    ext = cext_ref[...]
    acc = convb_ref[...] + convw_ref[CONV_WIDTH - 1:CONV_WIDTH, :] * ext
    for k in range(CONV_WIDTH - 1):
        acc = acc + convw_ref[k:k + 1, :] * pltpu.roll(ext, CONV_WIDTH - 1 - k, 0)
    xc = jnp.where(tok, _silu(acc), 0.0)
    xs = xc[:, :SSM_INNER]
    bm = xc[:, SSM_INNER:SSM_INNER + SSM_GROUPS * SSM_STATE]
    cm = xc[:, SSM_INNER + SSM_GROUPS * SSM_STATE:]

    dt = jnp.where(tok, _softplus(dtext_ref[...] + dtb_ref[...]), 0.0)
    la = dt * (-jnp.exp(alog_ref[...]))
    c1 = la + pltpu.roll(la, 1, 0)
    acum = c1 + pltpu.roll(c1, 2, 0)
    nxt = pltpu.roll(la, R - 1, 0)
    n1 = nxt + pltpu.roll(nxt, R - 1, 0)
    rest = n1 + pltpu.roll(n1, R - 2, 0)
    lane_g = lax.broadcasted_iota(jnp.int32, (R, SSM_GROUP_DIM), 1)
    lane_h = lax.broadcasted_iota(jnp.int32, (R, LANES), 1)

    xdt = xs * _expand_heads(dt, lane_g)
    xdt_q = xdt.astype(BF16).astype(F32)
    cm_q = cm.astype(BF16).astype(F32)
    bm_q = bm.astype(BF16).astype(F32)

    y_diag = jnp.zeros((R, SSM_INNER), F32)
    for k in range(n_tok):
        b_k = bm_q if k == 0 else pltpu.roll(bm_q, k, 0)
        prod = cm_q * b_k
        cb = jnp.zeros((R, LANES), F32)
        for g in reversed(range(SSM_GROUPS)):
            s_g = jnp.sum(prod[:, g * SSM_STATE:(g + 1) * SSM_STATE], axis=-1, keepdims=True)
            cb = jnp.where(lane_h < SSM_HEADS_PER_GROUP * (g + 1), s_g, cb)
        a_k = acum if k == 0 else pltpu.roll(acum, k, 0)
        decay = jnp.exp(jnp.where(t_idx >= k, acum - a_k, -jnp.inf))
        coef = (cb * decay).astype(BF16).astype(F32)
        x_k = xdt_q if k == 0 else pltpu.roll(xdt_q, k, 0)
        y_diag = y_diag + _expand_heads(coef, lane_g) * x_k

    xw_t = (xdt * _expand_heads(jnp.exp(rest), lane_g)).T
    total_t = jnp.exp(acum + rest).T
    bm_b = bm.astype(BF16)
    col = lax.broadcasted_iota(jnp.int32, (SSM_GROUP_DIM, R), 1)
    for b in range(SAMPLE_SEQ_TILE):
        r0 = b * SEQ_ROWS
        c_rows = cm[r0:r0 + SEQ_ROWS, :]
        in_seq = (col >= r0) & (col < r0 + SEQ_ROWS)
        last_col = r0 + SEQ_ROWS - 1
        for g in range(SSM_GROUPS):
            rows = slice(g * SSM_GROUP_DIM, (g + 1) * SSM_GROUP_DIM)
            h0 = st_ref[b, rows, :]
            c_g = c_rows[:, g * SSM_STATE:(g + 1) * SSM_STATE].astype(BF16)
            yoff_ref[r0:r0 + SEQ_ROWS, rows] = _dot_nt(c_g, h0.astype(BF16))
            lhs = jnp.where(in_seq, xw_t[rows, :], 0.0).astype(BF16)
            st = _dot(lhs, bm_b[:, g * SSM_STATE:(g + 1) * SSM_STATE])
            for j in range(SSM_HEADS_PER_GROUP):
                hh = SSM_HEADS_PER_GROUP * g + j
                hr = slice(j * SSM_HEAD_DIM, (j + 1) * SSM_HEAD_DIM)
                stn_ref[b, g * SSM_GROUP_DIM + j * SSM_HEAD_DIM:g * SSM_GROUP_DIM + (j + 1) * SSM_HEAD_DIM, :] = (
                    h0[hr, :] * total_t[hh:hh + 1, last_col:last_col + 1] + st[hr, :])

    y_ref[...] = (y_diag + yoff_ref[...] * _expand_heads(jnp.exp(acum), lane_g) + dskip_ref[...] * xs)


def _sample_ssd_chained_kernel(cext_ref, pext_ref, dtext_ref, st_ref, convw_ref, convb_ref, dtb_ref, alog_ref,
                              dskip_ref, poolw_ref, pscale_ref, prev_ref, *rest):
    del prev_ref
    _sample_ssd_kernel(cext_ref, pext_ref, dtext_ref, st_ref, convw_ref, convb_ref, dtb_ref, alog_ref,
                       dskip_ref, poolw_ref, pscale_ref, *rest)


def _sample_ssd_call(layer, cext, pext, dtext, state, new_state, p):
    n_seq = state.shape[1]
    R = SAMPLE_SEQ_TILE * SEQ_ROWS
    RP = SAMPLE_SEQ_TILE * POOL_SEQ_ROWS

    def vspec(rows, cols):
        return _const_spec((None, rows, cols), (layer, 0, 0))

    st_spec = pl.BlockSpec((None, SAMPLE_SEQ_TILE, SSM_INNER, SSM_STATE), lambda i: (layer, i, 0, 0))
    in_specs = [pl.BlockSpec((R, CONV_DIM), lambda i: (i, 0)),
                pl.BlockSpec((RP, POOL_DIM), lambda i: (i, 0)),
                pl.BlockSpec((R, LANES), lambda i: (i, 0)),
                st_spec,
                vspec(CONV_WIDTH, CONV_DIM), vspec(1, CONV_DIM), vspec(1, LANES), vspec(1, LANES),
                vspec(1, SSM_INNER),
                _const_spec((None, 4, POOL_GROUP_DIM, POOL_GROUP_DIM), (layer, 0, 0, 0)),
                vspec(1, POOL_DIM)]
    args = [cext, pext, dtext, state, p["conv_w"], p["conv_b"], p["dt_bias"], p["a_log"], p["d_skip"],
            p["pool_w"], p["pool_scale"]]
    kern, aliases = _sample_ssd_kernel, {}
    if new_state is not None:
        in_specs.append(pl.BlockSpec(memory_space=pl.ANY))
        args.append(new_state)
        kern, aliases = _sample_ssd_chained_kernel, {len(args) - 1: 2}
    return pl.pallas_call(
        kern,
        out_shape=(jax.ShapeDtypeStruct((n_seq * SEQ_ROWS, SSM_INNER), F32),
                   jax.ShapeDtypeStruct((n_seq * POOL_SEQ_ROWS, POOL_DIM), F32),
                   jax.ShapeDtypeStruct(state.shape, F32)),
        grid=(n_seq // SAMPLE_SEQ_TILE,),
        in_specs=in_specs,
        out_specs=(pl.BlockSpec((R, SSM_INNER), lambda i: (i, 0)),
                   pl.BlockSpec((RP, POOL_DIM), lambda i: (i, 0)),
                   st_spec),
        scratch_shapes=[pltpu.VMEM((R, SSM_INNER), F32)],
        input_output_aliases=aliases,
        compiler_params=pltpu.CompilerParams(dimension_semantics=("arbitrary",),
                                             vmem_limit_bytes=VMEM_LIMIT),
        name="sample_ssd",
    )(*args)


def _sample_attn_kernel(q_ref, k_ref, v_ref, o_ref, s_ref):
    n_mem = k_ref.shape[1] // MEM_HEADS
    HR = MEM_HEADS * SEQ_ROWS
    row_head = lax.broadcasted_iota(jnp.int32, (HR, MEM_DIM), 0) // SEQ_ROWS
    col_head = lax.broadcasted_iota(jnp.int32, (HR, MEM_DIM), 1) // MEM_HEAD_DIM
    own_head = row_head == col_head

    def heads_on_lanes(ref, b):
        return jnp.concatenate([ref[b, pl.ds(h, n_mem, stride=MEM_HEADS), :] for h in range(MEM_HEADS)],
                               axis=1).astype(BF16)

    for b in range(SAMPLE_ATTN_TILE):
        q = q_ref[b * SEQ_ROWS:(b + 1) * SEQ_ROWS, :]
        q_heads = jnp.where(own_head, jnp.concatenate([q] * MEM_HEADS, axis=0), 0.0).astype(BF16)
        s_ref[b * HR:(b + 1) * HR, :] = _dot_nt(q_heads, heads_on_lanes(k_ref, b))
    s = s_ref[...] * (MEM_HEAD_DIM ** -0.5)
    e = jnp.exp(s - jnp.max(s, axis=-1, keepdims=True))
    p = (e * (1.0 / jnp.sum(e, axis=-1, keepdims=True))).astype(BF16)
    for b in range(SAMPLE_ATTN_TILE):
        o = _dot(p[b * HR:(b + 1) * HR, :], heads_on_lanes(v_ref, b))
        o_ref[b * SEQ_ROWS:(b + 1) * SEQ_ROWS, :] = jnp.concatenate(
            [o[h * SEQ_ROWS:(h + 1) * SEQ_ROWS, h * MEM_HEAD_DIM:(h + 1) * MEM_HEAD_DIM]
             for h in range(MEM_HEADS)], axis=1)


def _sample_attn_call(layer, qext, cache_k, cache_v):
    n_seq, kv_rows = cache_k.shape[1], cache_k.shape[2]
    R = SAMPLE_ATTN_TILE * SEQ_ROWS
    kv_spec = pl.BlockSpec((None, SAMPLE_ATTN_TILE, kv_rows, MEM_HEAD_DIM), lambda i: (layer, i, 0, 0))
    return pl.pallas_call(
        _sample_attn_kernel,
        out_shape=jax.ShapeDtypeStruct((n_seq * SEQ_ROWS, MEM_DIM), F32),
        grid=(n_seq // SAMPLE_ATTN_TILE,),
        in_specs=[pl.BlockSpec((R, MEM_DIM), lambda i: (i, 0)), kv_spec, kv_spec],
        out_specs=pl.BlockSpec((R, MEM_DIM), lambda i: (i, 0)),
        scratch_shapes=[pltpu.VMEM((SAMPLE_ATTN_TILE * MEM_HEADS * SEQ_ROWS, kv_rows // MEM_HEADS), F32)],
        compiler_params=pltpu.CompilerParams(dimension_semantics=("arbitrary",),
                                             vmem_limit_bytes=VMEM_LIMIT),
        name="sample_attn",
    )(qext, cache_k, cache_v)


def _sample_merge_kernel(x_ref, z_ref, g_ref, y_ref, ypool_ref, ymem_ref, gbias_ref, snorm_ref,
                         wbp_ref, wbs_ref, wbm_ref, wo_ref, xo_ref):
    xo_ref[...] = _merge_tail(x_ref[...], _silu(z_ref[...]), jax.nn.sigmoid(g_ref[...] + gbias_ref[...]),
                              y_ref[...], ypool_ref[...], ymem_ref[...], snorm_ref[...],
                              wbp_ref, wbs_ref, wbm_ref, wo_ref)


def _sample_merge_call(x, layer, row_block, n_rows, z, graw, y_raw, y_pool, y_mem, p):
    def full(cols):
        return pl.BlockSpec((n_rows, cols), lambda i: (0, 0))

    def wspec(rows, cols):
        return _const_spec((None, rows, cols), (layer, 0, 0))

    x_spec = pl.BlockSpec((n_rows, D_MODEL), lambda i: (row_block, 0))
    return pl.pallas_call(
        _sample_merge_kernel,
        out_shape=jax.ShapeDtypeStruct(x.shape, F32),
        grid=(1,),
        in_specs=[x_spec, full(SSM_INNER), full(N_BRANCH * D_MODEL), full(SSM_INNER), full(POOL_DIM),
                  full(MEM_DIM), wspec(1, N_BRANCH * D_MODEL), wspec(1, SSM_INNER),
                  wspec(POOL_DIM, D_MODEL), wspec(SSM_INNER, D_MODEL), wspec(MEM_DIM, D_MODEL),
                  wspec(D_MODEL, D_MODEL)],
        out_specs=x_spec,
        input_output_aliases={0: 0},
        compiler_params=pltpu.CompilerParams(dimension_semantics=("arbitrary",),
                                             vmem_limit_bytes=VMEM_LIMIT),
        name="sample_merge",
    )(x, z, graw, y_raw, y_pool, y_mem, p["gate_bias"], p["ssm_norm"],
      p["w_br_pool"], p["w_br_ssm"], p["w_br_mem"], p["w_o"])


def _row3(a):
    return a.reshape(a.shape[0], 1, a.shape[1])


def _pad_lanes(a):
    return jnp.pad(a, [(0, 0)] * (a.ndim - 1) + [(0, LANES - a.shape[-1])])


def _token_rows(ext, rows_per_seq, n_tok):
    n_seq = ext.shape[0] // rows_per_seq
    return ext.reshape(n_seq, rows_per_seq, -1)[:, rows_per_seq - n_tok:, :].reshape(n_seq * n_tok, -1)


def kernel(x_prompt, x_sample, mem_prompt, state_pool, state_conv, state_ssm, cache_mem_k, cache_mem_v,
           ffn1_norm, ffn1_w_gate, ffn1_w_up, ffn1_w_down, mix_norm, w_in, gate_bias,
           pool_w, pool_scale, conv_w, conv_b, dt_bias, a_log, d_skip, ssm_norm,
           mem_norm, w_mem_k, w_mem_v, w_br_pool, w_br_ssm, w_br_mem, w_o,
           ffn2_norm, ffn2_w_gate, ffn2_w_up, ffn2_w_down, final_norm):
    depth = w_in.shape[0]
    batch, seq, _ = x_prompt.shape
    n_seq, n_tok, _ = x_sample.shape
    n_mem = mem_prompt.shape[1]
    n_prompt = batch * seq
    n_sample = n_seq * n_tok
    assert n_sample == FFN_TOKEN_TILE and n_prompt % FFN_TOKEN_TILE == 0 and seq % PROMPT_CHUNK == 0
    assert n_tok == SEQ_ROWS // 2 and PAST_LEN >= max(POOL_WINDOWS)
    assert n_seq % SAMPLE_SEQ_TILE == 0 and n_seq % SAMPLE_ATTN_TILE == 0
    sample_block = n_prompt // n_sample

    w_pool, w_z, w_xbc, w_dt, w_q, w_g = _split_w_in_call(w_in)
    params = dict(
        mix_norm=_row3(mix_norm),
        w_pool=w_pool, w_z=w_z, w_xbc=w_xbc, w_dt=w_dt, w_q=w_q, w_g=w_g,
        gate_bias=_row3(gate_bias), pool_w=pool_w.astype(BF16), pool_scale=_row3(pool_scale),
        conv_w=conv_w, conv_b=_row3(conv_b), dt_bias=_row3(_pad_lanes(dt_bias)), a_log=_row3(_pad_lanes(a_log)),
        d_skip=_row3(jnp.repeat(d_skip, SSM_HEAD_DIM, axis=1)), ssm_norm=_row3(ssm_norm),
        w_br_pool=w_br_pool.astype(BF16), w_br_ssm=w_br_ssm.astype(BF16), w_br_mem=w_br_mem.astype(BF16),
        w_o=w_o.astype(BF16))
    ffn1 = (_row3(ffn1_norm), ffn1_w_gate.astype(BF16), ffn1_w_up.astype(BF16), ffn1_w_down.astype(BF16))
    ffn2 = (_row3(ffn2_norm), ffn2_w_gate.astype(BF16), ffn2_w_up.astype(BF16), ffn2_w_down.astype(BF16))

    mem_k, mem_v, mem_kb, mem_vb = _memkv_call(mem_prompt, _row3(mem_norm), w_mem_k.astype(BF16),
                                               w_mem_v.astype(BF16))
    cache_k = cache_mem_k.reshape(depth, n_seq, n_mem * MEM_HEADS, MEM_HEAD_DIM)
    cache_v = cache_mem_v.reshape(depth, n_seq, n_mem * MEM_HEADS, MEM_HEAD_DIM)
    state_ssm_rows = state_ssm.reshape(depth, n_seq, SSM_INNER, SSM_STATE)

    pool_p, conv_p, ssm_p, pool_s, conv_s = [], [], [], [], []
    ssm_s = None
    for l in range(depth):
        if l == 0:
            x = _ffn_first_call(x_prompt.reshape(n_prompt, D_MODEL), x_sample.reshape(n_sample, D_MODEL), l, ffn1)
        else:
            x = _ffn_call(x, l, ffn1)
        x, npool, nconv, nssm = _prompt_mixer_call(x, l, batch, seq, params, mem_kb, mem_vb)
        pool_p.append(npool)
        conv_p.append(nconv)
        ssm_p.append(nssm.reshape(batch, SSM_HEADS, SSM_HEAD_DIM, SSM_STATE))

        u, z, xbc, dt_raw, q, graw = _sample_inproj_call(x, l, sample_block, n_sample, params)
        u3 = u.reshape(n_seq, n_tok, POOL_DIM)
        xbc3 = xbc.reshape(n_seq, n_tok, CONV_DIM)
        pext = jnp.concatenate([jnp.zeros((n_seq, POOL_SEQ_ROWS - POOL_BUF - n_tok, POOL_DIM), F32),
                                state_pool[l], u3], axis=1).reshape(n_seq * POOL_SEQ_ROWS, POOL_DIM)
        cext = jnp.concatenate([jnp.zeros((n_seq, SEQ_ROWS - CONV_WIDTH + 1 - n_tok, CONV_DIM), F32),
                                state_conv[l], xbc3], axis=1).reshape(n_seq * SEQ_ROWS, CONV_DIM)

        def pad_seq(a):
            a3 = a.reshape(n_seq, n_tok, a.shape[-1])
            return jnp.concatenate([jnp.zeros_like(a3), a3], axis=1).reshape(n_seq * SEQ_ROWS, a.shape[-1])

        y_ext, ypool_ext, ssm_s = _sample_ssd_call(l, cext, pext, pad_seq(dt_raw), state_ssm_rows, ssm_s, params)
        ymem_ext = _sample_attn_call(l, pad_seq(q), cache_k, cache_v)
        x = _sample_merge_call(x, l, sample_block, n_sample, z, graw,
                               _token_rows(y_ext, SEQ_ROWS, n_tok), _token_rows(ypool_ext, POOL_SEQ_ROWS, n_tok),
                               _token_rows(ymem_ext, SEQ_ROWS, n_tok), params)
        pool_s.append(jnp.concatenate([state_pool[l][:, n_tok:], u3], axis=1))
        conv_s.append(xbc3[:, n_tok - (CONV_WIDTH - 1):])

        if l < depth - 1:
            x = _ffn_call(x, l, ffn2)
    y_prompt, y_sample = _ffn_final_call(x, n_prompt, depth - 1, ffn2, final_norm.reshape(1, D_MODEL))

    mem_shape = (depth, batch, n_mem, MEM_HEADS, MEM_HEAD_DIM)
    return (y_prompt.reshape(batch, seq, D_MODEL), y_sample.reshape(n_seq, n_tok, D_MODEL),
            jnp.stack(pool_p), jnp.stack(conv_p), jnp.stack(ssm_p),
            mem_k.reshape(mem_shape), mem_v.reshape(mem_shape),
            jnp.stack(pool_s), jnp.stack(conv_s),
            ssm_s.reshape(depth, n_seq, SSM_HEADS, SSM_HEAD_DIM, SSM_STATE))
```

```python
import functools

import jax
import jax.numpy as jnp
from jax import lax
from jax.experimental import pallas as pl
from jax.experimental.pallas import tpu as pltpu

F32 = jnp.float32
BF16 = jnp.bfloat16

D_MODEL = 1024
D_FF = 2816
EPS = 1e-6
PAST_LEN = 16384

POOL_WINDOWS = (2, 4, 8, 16)
POOL_GROUP_DIM = 128
POOL_DIM = 512
POOL_BUF = 15
SSM_INNER = 1024
SSM_HEADS = 16
SSM_HEAD_DIM = 64
SSM_GROUPS = 4
SSM_HEADS_PER_GROUP = 4
SSM_GROUP_DIM = 256
SSM_STATE = 128
CONV_WIDTH = 4
CONV_DIM = 2048
MEM_HEADS = 4
MEM_HEAD_DIM = 128
MEM_DIM = 512
N_BRANCH = 3
IN_SPLITS = (POOL_DIM, SSM_INNER, CONV_DIM, SSM_HEADS, MEM_DIM, N_BRANCH * D_MODEL)

SUBLANES = 8
LANES = 128

FFN_TOKEN_TILE = 512
FFN_FF_CHUNK = 2816
PROMPT_CHUNK = 256
SAMPLE_SEQ_TILE = 16
MEMKV_BATCH_TILE = 4
W_IN_ROW_TILE = 256
SEQ_ROWS = 8
POOL_SEQ_ROWS = 24
VMEM_LIMIT = 56 * 1024 * 1024

_NT = (((1,), (1,)), ((), ()))
_TN = (((0,), (0,)), ((), ()))


def _dot(a, b):
    return jnp.dot(a, b, preferred_element_type=F32)


def _dot_nt(a, b):
    return lax.dot_general(a, b, _NT, preferred_element_type=F32)


def _rmsnorm(x, g):
    return x * lax.rsqrt(jnp.mean(x * x, axis=-1, keepdims=True) + EPS) * g


def _silu(x):
    return x * jax.nn.sigmoid(x)


def _softplus(x):
    return jnp.maximum(x, 0.0) + jnp.log1p(jnp.exp(-jnp.abs(x)))


def _pitch(cols):
    return cols + LANES if cols % (SUBLANES * LANES) == 0 else cols


def _const_spec(shape, index, single_buffer=True):
    mode = pl.Buffered(1) if single_buffer else None
    return pl.BlockSpec(shape, lambda *_: index, pipeline_mode=mode)


def _expand_group(v, g, lane):
    h0 = SSM_HEADS_PER_GROUP * g
    out = jnp.broadcast_to(v[:, h0 + 3:h0 + 4], lane.shape)
    for j in (2, 1, 0):
        out = jnp.where(lane < SSM_HEAD_DIM * (j + 1), v[:, h0 + j:h0 + j + 1], out)
    return out


def _expand_heads(v, lane):
    return jnp.concatenate([_expand_group(v, g, lane) for g in range(SSM_GROUPS)], axis=1)


def _ffn_body(x, g, wg_ref, wu_ref, wd_ref):
    hn = _rmsnorm(x, g).astype(BF16)
    acc = None
    for j in range(D_FF // FFN_FF_CHUNK):
        sl = slice(j * FFN_FF_CHUNK, (j + 1) * FFN_FF_CHUNK)
        gate = _dot(hn, wg_ref[:, sl])
        up = _dot(hn, wu_ref[:, sl])
        part = _dot((_silu(gate) * up).astype(BF16), wd_ref[sl, :])
        acc = part if acc is None else acc + part
    return x + 0.5 * acc


def _ffn_kernel(x_ref, g_ref, wg_ref, wu_ref, wd_ref, o_ref):
    o_ref[...] = _ffn_body(x_ref[...], g_ref[...], wg_ref, wu_ref, wd_ref)


def _ffn_first_kernel(xp_ref, xs_ref, g_ref, wg_ref, wu_ref, wd_ref, o_ref, *, prompt_blocks):
    x = jnp.where(pl.program_id(0) < prompt_blocks, xp_ref[...], xs_ref[...])
    o_ref[...] = _ffn_body(x, g_ref[...], wg_ref, wu_ref, wd_ref)


def _ffn_final_kernel(x_ref, g_ref, wg_ref, wu_ref, wd_ref, fg_ref, yp_ref, ys_ref, *, prompt_blocks):
    y = _rmsnorm(_ffn_body(x_ref[...], g_ref[...], wg_ref, wu_ref, wd_ref), fg_ref[...])
    i = pl.program_id(0)

    @pl.when(i < prompt_blocks)
    def _():
        yp_ref[...] = y

    @pl.when(i == prompt_blocks)
    def _():
        ys_ref[...] = y


def _ffn_weight_specs(layer):
    return [_const_spec((None, 1, D_MODEL), (layer, 0, 0)),
            _const_spec((None, D_MODEL, D_FF), (layer, 0, 0)),
            _const_spec((None, D_MODEL, D_FF), (layer, 0, 0)),
            _const_spec((None, D_FF, D_MODEL), (layer, 0, 0))]


_FFN_PARAMS = pltpu.CompilerParams(dimension_semantics=("arbitrary",), vmem_limit_bytes=VMEM_LIMIT)


def _ffn_call(x, layer, weights):
    tile = pl.BlockSpec((FFN_TOKEN_TILE, D_MODEL), lambda i: (i, 0))
    return pl.pallas_call(
        _ffn_kernel,
        out_shape=jax.ShapeDtypeStruct(x.shape, F32),
        grid=(x.shape[0] // FFN_TOKEN_TILE,),
        in_specs=[tile] + _ffn_weight_specs(layer),
        out_specs=tile,
        input_output_aliases={0: 0},
        compiler_params=_FFN_PARAMS,
        name="ffn",
    )(x, *weights)


def _ffn_first_call(x_prompt, x_sample, layer, weights):
    prompt_blocks = x_prompt.shape[0] // FFN_TOKEN_TILE
    n_tok = x_prompt.shape[0] + x_sample.shape[0]
    tile = pl.BlockSpec((FFN_TOKEN_TILE, D_MODEL), lambda i: (i, 0))
    p_tile = pl.BlockSpec((FFN_TOKEN_TILE, D_MODEL), lambda i: (jnp.minimum(i, prompt_blocks - 1), 0))
    s_tile = pl.BlockSpec((FFN_TOKEN_TILE, D_MODEL), lambda i: (0, 0))
    return pl.pallas_call(
        functools.partial(_ffn_first_kernel, prompt_blocks=prompt_blocks),
        out_shape=jax.ShapeDtypeStruct((n_tok, D_MODEL), F32),
        grid=(n_tok // FFN_TOKEN_TILE,),
        in_specs=[p_tile, s_tile] + _ffn_weight_specs(layer),
        out_specs=tile,
        compiler_params=_FFN_PARAMS,
        name="ffn_first",
    )(x_prompt, x_sample, *weights)


def _ffn_final_call(x, n_prompt, layer, weights, final_norm):
    prompt_blocks = n_prompt // FFN_TOKEN_TILE
    n_tok = x.shape[0]
    tile = pl.BlockSpec((FFN_TOKEN_TILE, D_MODEL), lambda i: (i, 0))
    p_tile = pl.BlockSpec((FFN_TOKEN_TILE, D_MODEL), lambda i: (jnp.minimum(i, prompt_blocks - 1), 0))
    s_tile = pl.BlockSpec((FFN_TOKEN_TILE, D_MODEL), lambda i: (0, 0))
    return pl.pallas_call(
        functools.partial(_ffn_final_kernel, prompt_blocks=prompt_blocks),
        out_shape=(jax.ShapeDtypeStruct((n_prompt, D_MODEL), F32),
                   jax.ShapeDtypeStruct((n_tok - n_prompt, D_MODEL), F32)),
        grid=(n_tok // FFN_TOKEN_TILE,),
        in_specs=[tile] + _ffn_weight_specs(layer) + [_const_spec((1, D_MODEL), (0, 0))],
        out_specs=(p_tile, s_tile),
        compiler_params=_FFN_PARAMS,
        name="ffn_final",
    )(x, *weights, final_norm)


def _split_w_in_kernel(wt_ref, *out_refs):
    start = 0
    for o_ref, cols in zip(out_refs, IN_SPLITS):
        stored = o_ref.shape[-1]
        if cols >= LANES:
            o_ref[:, :cols] = wt_ref[start:start + cols, :].T.astype(BF16)
            if stored > cols:
                o_ref[:, cols:] = jnp.zeros((o_ref.shape[0], stored - cols), BF16)
        else:
            tile = wt_ref[start:start + stored, :].T
            lane = lax.broadcasted_iota(jnp.int32, tile.shape, 1)
            o_ref[...] = jnp.where(lane < cols, tile, 0.0).astype(BF16)
        start += cols


def _split_w_in_call(w_in):
    depth, rows, total = w_in.shape
    stored = [_pitch(c) if c >= LANES else LANES for c in IN_SPLITS]
    tile = W_IN_ROW_TILE
    return pl.pallas_call(
        _split_w_in_kernel,
        out_shape=tuple(jax.ShapeDtypeStruct((depth, rows, s), BF16) for s in stored),
        grid=(depth, rows // tile),
        in_specs=[pl.BlockSpec((None, total, tile), lambda l, r: (l, 0, r))],
        out_specs=tuple(pl.BlockSpec((None, tile, s), lambda l, r: (l, r, 0)) for s in stored),
        compiler_params=pltpu.CompilerParams(dimension_semantics=("arbitrary", "arbitrary"),
                                             vmem_limit_bytes=VMEM_LIMIT),
        name="split_w_in",
    )(jnp.swapaxes(w_in, 1, 2))


def _memkv_kernel(m_ref, g_ref, wk_ref, wv_ref, k_ref, v_ref, kb_ref, vb_ref):
    nb, n_mem, _ = m_ref.shape
    hn = _rmsnorm(m_ref[...].reshape(nb * n_mem, D_MODEL), g_ref[...]).astype(BF16)
    k = _dot(hn, wk_ref[...])
    v = _dot(hn, wv_ref[...])
    for b in range(nb):
        rows = slice(b * n_mem, (b + 1) * n_mem)
        for h in range(MEM_HEADS):
            hs = slice(h * MEM_HEAD_DIM, (h + 1) * MEM_HEAD_DIM)
            k_ref[b, pl.ds(h, n_mem, stride=MEM_HEADS), :] = k[rows, hs]
            v_ref[b, pl.ds(h, n_mem, stride=MEM_HEADS), :] = v[rows, hs]
        kb_ref[b] = k[rows, :].astype(BF16)
        vb_ref[b] = v[rows, :].astype(BF16)


def _memkv_call(mem, norm, wk, wv):
    depth = norm.shape[0]
    batch, n_mem, _ = mem.shape
    nb = MEMKV_BATCH_TILE
    f32_spec = pl.BlockSpec((None, nb, n_mem * MEM_HEADS, MEM_HEAD_DIM), lambda l, b: (l, b, 0, 0))
    out_spec = pl.BlockSpec((None, nb, n_mem, MEM_DIM), lambda l, b: (l, b, 0, 0))
    w_spec = pl.BlockSpec((None, D_MODEL, MEM_DIM), lambda l, b: (l, 0, 0))
    f32_out = jax.ShapeDtypeStruct((depth, batch, n_mem * MEM_HEADS, MEM_HEAD_DIM), F32)
    bf16_out = jax.ShapeDtypeStruct((depth, batch, n_mem, MEM_DIM), BF16)
    return pl.pallas_call(
        _memkv_kernel,
        out_shape=(f32_out, f32_out, bf16_out, bf16_out),
        grid=(depth, batch // nb),
        in_specs=[pl.BlockSpec((nb, n_mem, D_MODEL), lambda l, b: (b, 0, 0)),
                  pl.BlockSpec((None, 1, D_MODEL), lambda l, b: (l, 0, 0)),
                  w_spec, w_spec],
        out_specs=(f32_spec, f32_spec, out_spec, out_spec),
        compiler_params=pltpu.CompilerParams(dimension_semantics=("arbitrary", "arbitrary"),
                                             vmem_limit_bytes=VMEM_LIMIT),
        name="mem_kv",
    )(mem, norm, wk, wv)


def _merge_tail(x, silu_z, gates, y_raw, y_pool, y_mem, snorm, wbp_ref, wbs_ref, wbm_ref, wo_ref):
    yz = y_raw * silu_z
    parts = []
    for g in range(SSM_GROUPS):
        yg = yz[:, g * SSM_GROUP_DIM:(g + 1) * SSM_GROUP_DIM]
        parts.append(yg * lax.rsqrt(jnp.mean(yg * yg, axis=-1, keepdims=True) + EPS))
    y_ssm = (jnp.concatenate(parts, axis=1) * snorm).astype(BF16)
    merged = (gates[:, :D_MODEL] * _dot(y_pool.astype(BF16), wbp_ref[:, :D_MODEL])
              + gates[:, D_MODEL:2 * D_MODEL] * _dot(y_ssm, wbs_ref[:, :D_MODEL])
              + gates[:, 2 * D_MODEL:] * _dot(y_mem.astype(BF16), wbm_ref[:, :D_MODEL]))
    return x + _dot(merged.astype(BF16), wo_ref[:, :D_MODEL])


def _attend(q, k, v):
    s = _dot_nt(q, k) * (MEM_HEAD_DIM ** -0.5)
    e = jnp.exp(s - jnp.max(s, axis=-1, keepdims=True))
    p = e * (1.0 / jnp.sum(e, axis=-1, keepdims=True))
    return _dot(p.astype(BF16), v)


def _prompt_mixer_kernel(x_ref, norm_ref, wpool_ref, wz_ref, wxbc_ref, wdt_ref, wq_ref, wg_ref,
                         gbias_ref, poolw_ref, pscale_ref, convw_ref, convb_ref, dtb_ref, alog_ref,
                         dskip_ref, snorm_ref, wbp_ref, wbs_ref, wbm_ref, wo_ref, k_ref, v_ref,
                         sq_ref, sk_ref, sv_ref,
                         xo_ref, poolst_ref, convst_ref, h_ref, so_ref,
                         cbuf_ref, pbuf_ref, ss_ref):
    T = PROMPT_CHUNK
    W = SSM_GROUP_DIM
    c = pl.program_id(1)

    _sample_attn_kernel(sq_ref, sk_ref, sv_ref, so_ref, ss_ref)

    @pl.when(c == 0)
    def _():
        h_ref[...] = jnp.zeros_like(h_ref)
        cbuf_ref[0:SUBLANES, :] = jnp.zeros((SUBLANES, CONV_DIM), F32)
        pbuf_ref[0:2 * SUBLANES, :] = jnp.zeros((2 * SUBLANES, POOL_DIM), F32)

    x = x_ref[...]
    hn = _rmsnorm(x, norm_ref[...]).astype(BF16)

    dt = _softplus(_dot(hn, wdt_ref[...]) + dtb_ref[...])
    la = dt * (-jnp.exp(alog_ref[...]))
    la_t = la.T[0:SSM_HEADS, :]
    lane_t = lax.broadcasted_iota(jnp.int32, (SSM_HEADS, T), 1)
    acum_t = la_t
    shift = 1
    while shift < T:
        acum_t = acum_t + jnp.where(lane_t >= shift, pltpu.roll(acum_t, shift, 1), 0.0)
        shift *= 2
    acum = jnp.concatenate([acum_t, jnp.zeros((LANES - SSM_HEADS, T), F32)], axis=0).T
    from_start = jnp.exp(acum)
    to_end = jnp.exp(acum[T - 1:T, :] - acum)
    chunk_decay = jnp.exp(acum_t[:, T - 1:T])

    P0 = 2 * SUBLANES
    u = _dot(hn, wpool_ref[...])
    pbuf_ref[P0:P0 + T, :] = u
    pos = c * T + lax.broadcasted_iota(jnp.int32, (T, 1), 0)

    C0 = SUBLANES
    n_blocks = CONV_DIM // W
    n_x = SSM_INNER // W
    xc = [None] * n_blocks
    wide_out = [None] * n_blocks
    y_pool = [None] * len(POOL_WINDOWS)
    q = _dot(hn, wq_ref[...])
    lane_g = lax.broadcasted_iota(jnp.int32, (T, W), 1)
    causal = (lax.broadcasted_iota(jnp.int32, (T, T), 0) >= lax.broadcasted_iota(jnp.int32, (T, T), 1))

    def wide_block(i):
        lo, hi = 2 * i * W, 2 * (i + 1) * W
        if hi <= N_BRANCH * D_MODEL:
            wide_out[i] = jax.nn.sigmoid(_dot(hn, wg_ref[:, lo:hi]) + gbias_ref[:, lo:hi])
        else:
            lo, hi = lo - N_BRANCH * D_MODEL, hi - N_BRANCH * D_MODEL
            wide_out[i] = _silu(_dot(hn, wz_ref[:, lo:hi]))

    def conv_block(j):
        cols = slice(j * W, (j + 1) * W)
        xbc = _dot(hn, wxbc_ref[:, cols])
        cbuf_ref[C0:C0 + T, cols] = xbc
        acc = convb_ref[:, cols] + convw_ref[CONV_WIDTH - 1:CONV_WIDTH, cols] * xbc
        for k in range(CONV_WIDTH - 1):
            off = C0 - (CONV_WIDTH - 1) + k
            acc = acc + convw_ref[k:k + 1, cols] * cbuf_ref[off:off + T, cols]
        xc[j] = _silu(acc)
        convst_ref[:, cols] = xbc[T - (CONV_WIDTH - 1):, :]
        cbuf_ref[0:C0, cols] = xbc[T - C0:, :]

    def pool_group(j):
        w = POOL_WINDOWS[j]
        cs = slice(j * POOL_GROUP_DIM, (j + 1) * POOL_GROUP_DIM)
        ug = u[:, cs]
        s = ug
        for k in range(1, w):
            s = s + pbuf_ref[P0 - k:P0 - k + T, cs]
        inv_cnt = 1.0 / jnp.minimum(pos + 1, w).astype(F32)
        y_pool[j] = _dot((s * inv_cnt - ug).astype(BF16), poolw_ref[j])

    def ssd_group(g):
        half = slice((g % 2) * SSM_STATE, (g % 2 + 1) * SSM_STATE)
        xs_g = xc[g]
        bm_g = xc[n_x + g // 2][:, half].astype(BF16)
        cm_g = xc[n_x + SSM_GROUPS // 2 + g // 2][:, half].astype(BF16)
        cb = _dot_nt(cm_g, bm_g)
        xdt = xs_g * _expand_group(dt, g, lane_g)
        xdt_b = xdt.astype(BF16)
        lhs, rhs = [], []
        for j in range(SSM_HEADS_PER_GROUP):
            hh = SSM_HEADS_PER_GROUP * g + j
            seg = acum[:, hh:hh + 1] - acum_t[hh:hh + 1, :]
            decay = jnp.exp(jnp.where(causal, seg, -jnp.inf))
            lhs.append((cb * decay).astype(BF16))
            in_head = (lane_g >= SSM_HEAD_DIM * j) & (lane_g < SSM_HEAD_DIM * (j + 1))
            rhs.append(jnp.where(in_head, xdt_b, jnp.zeros_like(xdt_b)))
        y_diag = _dot(jnp.concatenate(lhs, axis=1), jnp.concatenate(rhs, axis=0))
        rows = slice(g * W, (g + 1) * W)
        h_prev = h_ref[rows, :]
        y_off = _dot_nt(cm_g, h_prev.astype(BF16)) * _expand_group(from_start, g, lane_g)
        xw = (xdt * _expand_group(to_end, g, lane_g)).astype(BF16)
        st = lax.dot_general(xw, bm_g, _TN, preferred_element_type=F32)
        for j in range(SSM_HEADS_PER_GROUP):
            hh = SSM_HEADS_PER_GROUP * g + j
            hr = slice(j * SSM_HEAD_DIM, (j + 1) * SSM_HEAD_DIM)
            h_ref[g * W + j * SSM_HEAD_DIM:g * W + (j + 1) * SSM_HEAD_DIM, :] = (
                h_prev[hr, :] * chunk_decay[hh:hh + 1, :] + st[hr, :])
        return y_diag + y_off + dskip_ref[:, rows] * xs_g

    def attend_head(h):
        hs = slice(h * MEM_HEAD_DIM, (h + 1) * MEM_HEAD_DIM)
        return _attend(q[:, hs].astype(BF16), k_ref[:, hs], v_ref[:, hs])

    for j in range(n_blocks):
        wide_block(j)
        conv_block(j)
        if j < len(POOL_WINDOWS):
            pool_group(j)
    y_groups, y_mem = [], []
    for g in range(SSM_GROUPS):
        y_groups.append(ssd_group(g))
        y_mem.append(attend_head(g))
    y_pool = jnp.concatenate(y_pool, axis=1) * pscale_ref[...]
    poolst_ref[...] = u[T - POOL_BUF:, :]
    pbuf_ref[0:P0, :] = u[T - P0:, :]
    n_gate = N_BRANCH * D_MODEL // (2 * W)
    gates = jnp.concatenate(wide_out[:n_gate], axis=1)
    silu_z = jnp.concatenate(wide_out[n_gate:], axis=1)

    xo_ref[...] = _merge_tail(x, silu_z, gates,
                              jnp.concatenate(y_groups, axis=1), y_pool, jnp.concatenate(y_mem, axis=1),
                              snorm_ref[...], wbp_ref, wbs_ref, wbm_ref, wo_ref)


def _prompt_mixer_call(x, layer, batch, seq, p, kb, vb, sample_q, cache_k, cache_v):
    T = PROMPT_CHUNK
    n_chunks = seq // T
    n_mem = kb.shape[2]
    x_spec = pl.BlockSpec((T, D_MODEL), lambda b, c: (b * n_chunks + c, 0))

    def wspec(rows, cols):
        return _const_spec((None, rows, cols), (layer, 0, 0))

    def vspec(cols):
        return _const_spec((None, 1, cols), (layer, 0, 0))

    kv_spec = pl.BlockSpec((None, None, n_mem, MEM_DIM), lambda b, c: (layer, b, 0, 0))
    n_seq, kv_rows = cache_k.shape[1], cache_k.shape[2]
    ride = n_seq // (batch * n_chunks)
    assert ride * batch * n_chunks == n_seq
    sq_spec = pl.BlockSpec((ride * SEQ_ROWS, MEM_DIM), lambda b, c: (b * n_chunks + c, 0))
    skv_spec = pl.BlockSpec((None, ride, kv_rows, MEM_HEAD_DIM), lambda b, c: (layer, b * n_chunks + c, 0, 0))
    in_specs = [x_spec, vspec(D_MODEL),
                wspec(D_MODEL, POOL_DIM), wspec(D_MODEL, _pitch(SSM_INNER)), wspec(D_MODEL, _pitch(CONV_DIM)),
                wspec(D_MODEL, LANES), wspec(D_MODEL, MEM_DIM), wspec(D_MODEL, _pitch(N_BRANCH * D_MODEL)),
                vspec(N_BRANCH * D_MODEL),
                _const_spec((None, 4, POOL_GROUP_DIM, POOL_GROUP_DIM), (layer, 0, 0, 0)),
                vspec(POOL_DIM), wspec(CONV_WIDTH, CONV_DIM), vspec(CONV_DIM), vspec(LANES), vspec(LANES),
                vspec(SSM_INNER), vspec(SSM_INNER),
                wspec(POOL_DIM, D_MODEL), wspec(SSM_INNER, D_MODEL), wspec(MEM_DIM, D_MODEL),
                wspec(D_MODEL, D_MODEL), kv_spec, kv_spec, sq_spec, skv_spec, skv_spec]
    out_shape = (jax.ShapeDtypeStruct(x.shape, F32),
                 jax.ShapeDtypeStruct((batch, POOL_BUF, POOL_DIM), F32),
                 jax.ShapeDtypeStruct((batch, CONV_WIDTH - 1, CONV_DIM), F32),
                 jax.ShapeDtypeStruct((batch, SSM_INNER, SSM_STATE), F32),
                 jax.ShapeDtypeStruct((n_seq * SEQ_ROWS, MEM_DIM), F32))
    out_specs = (x_spec,
                 pl.BlockSpec((None, POOL_BUF, POOL_DIM), lambda b, c: (b, 0, 0)),
                 pl.BlockSpec((None, CONV_WIDTH - 1, CONV_DIM), lambda b, c: (b, 0, 0)),
                 pl.BlockSpec((None, SSM_INNER, SSM_STATE), lambda b, c: (b, 0, 0)),
                 sq_spec)
    return pl.pallas_call(
        _prompt_mixer_kernel,
        out_shape=out_shape,
        grid=(batch, n_chunks),
        in_specs=in_specs,
        out_specs=out_specs,
        scratch_shapes=[pltpu.VMEM((SUBLANES + T, CONV_DIM), F32),
                        pltpu.VMEM((2 * SUBLANES + T, POOL_DIM), F32),
                        pltpu.VMEM((ride * MEM_HEADS * SEQ_ROWS, kv_rows // MEM_HEADS), F32)],
        input_output_aliases={0: 0},
        compiler_params=pltpu.CompilerParams(dimension_semantics=("arbitrary", "arbitrary"),
                                             vmem_limit_bytes=VMEM_LIMIT),
        name="prompt_mixer",
    )(x, p["mix_norm"], p["w_pool"], p["w_z"], p["w_xbc"], p["w_dt"], p["w_q"], p["w_g"],
      p["gate_bias"], p["pool_w"], p["pool_scale"], p["conv_w"], p["conv_b"], p["dt_bias"], p["a_log"],
      p["d_skip"], p["ssm_norm"], p["w_br_pool"], p["w_br_ssm"], p["w_br_mem"], p["w_o"], kb, vb,
      sample_q, cache_k, cache_v)


def _sample_inproj_kernel(x_ref, norm_ref, wpool_ref, wz_ref, wxbc_ref, wdt_ref, wq_ref, wg_ref,
                          u_ref, z_ref, xbc_ref, dt_ref, q_ref, g_ref):
    hn = _rmsnorm(x_ref[...], norm_ref[...]).astype(BF16)
    u_ref[...] = _dot(hn, wpool_ref[...])
    z_ref[...] = _dot(hn, wz_ref[:, :SSM_INNER])
    xbc_ref[...] = _dot(hn, wxbc_ref[:, :CONV_DIM])
    dt_ref[...] = _dot(hn, wdt_ref[...])
    q_ref[...] = _dot(hn, wq_ref[...])
    g_ref[...] = _dot(hn, wg_ref[:, :N_BRANCH * D_MODEL])


def _sample_inproj_call(x, layer, row_block, n_rows, p):
    def wspec(cols):
        return _const_spec((None, D_MODEL, cols), (layer, 0, 0))

    widths = (POOL_DIM, SSM_INNER, CONV_DIM, LANES, MEM_DIM, N_BRANCH * D_MODEL)
    return pl.pallas_call(
        _sample_inproj_kernel,
        out_shape=tuple(jax.ShapeDtypeStruct((n_rows, w), F32) for w in widths),
        grid=(1,),
        in_specs=[pl.BlockSpec((n_rows, D_MODEL), lambda i: (row_block, 0)),
                  _const_spec((None, 1, D_MODEL), (layer, 0, 0)),
                  wspec(POOL_DIM), wspec(_pitch(SSM_INNER)), wspec(_pitch(CONV_DIM)), wspec(LANES), wspec(MEM_DIM),
                  wspec(_pitch(N_BRANCH * D_MODEL))],
        out_specs=tuple(pl.BlockSpec((n_rows, w), lambda i: (0, 0)) for w in widths),
        compiler_params=pltpu.CompilerParams(dimension_semantics=("arbitrary",),
                                             vmem_limit_bytes=VMEM_LIMIT),
        name="sample_inproj",
    )(x, p["mix_norm"], p["w_pool"], p["w_z"], p["w_xbc"], p["w_dt"], p["w_q"], p["w_g"])


def _sample_ssd_kernel(cext_ref, pext_ref, dtext_ref, st_ref, convw_ref, convb_ref, dtb_ref, alog_ref,
                       dskip_ref, poolw_ref, pscale_ref,
                       y_ref, ypool_ref, stn_ref, yoff_ref):
    R = SAMPLE_SEQ_TILE * SEQ_ROWS
    n_tok = SEQ_ROWS // 2
    row = lax.broadcasted_iota(jnp.int32, (R, 1), 0)
    t_idx = row % SEQ_ROWS - n_tok
    tok = t_idx >= 0

    f = pext_ref[...]
    sums = []
    s = f
    for step in (1, 2, 4, 8):
        s = s + pltpu.roll(s, step, 0)
        sums.append(s)
    y_parts = []
    for g, w in enumerate(POOL_WINDOWS):
        cs = slice(g * POOL_GROUP_DIM, (g + 1) * POOL_GROUP_DIM)
        d = (sums[g][:, cs] * (1.0 / w) - f[:, cs]).astype(BF16)
        y_parts.append(_dot(d, poolw_ref[g]))
    ypool_ref[...] = jnp.concatenate(y_parts, axis=1) * pscale_ref[...]

    ext = cext_ref[...]
    acc = convb_ref[...] + convw_ref[CONV_WIDTH - 1:CONV_WIDTH, :] * ext
    for k in range(CONV_WIDTH - 1):
        acc = acc + convw_ref[k:k + 1, :] * pltpu.roll(ext, CONV_WIDTH - 1 - k, 0)
    xc = jnp.where(tok, _silu(acc), 0.0)
    xs = xc[:, :SSM_INNER]
    bm = xc[:, SSM_INNER:SSM_INNER + SSM_GROUPS * SSM_STATE]
    cm = xc[:, SSM_INNER + SSM_GROUPS * SSM_STATE:]

    dt = jnp.where(tok, _softplus(dtext_ref[...] + dtb_ref[...]), 0.0)
    la = dt * (-jnp.exp(alog_ref[...]))
    c1 = la + pltpu.roll(la, 1, 0)
    acum = c1 + pltpu.roll(c1, 2, 0)
    nxt = pltpu.roll(la, R - 1, 0)
    n1 = nxt + pltpu.roll(nxt, R - 1, 0)
    rest = n1 + pltpu.roll(n1, R - 2, 0)
    lane_g = lax.broadcasted_iota(jnp.int32, (R, SSM_GROUP_DIM), 1)
    lane_h = lax.broadcasted_iota(jnp.int32, (R, LANES), 1)

    xdt = xs * _expand_heads(dt, lane_g)
    xdt_q = xdt.astype(BF16).astype(F32)
    cm_q = cm.astype(BF16).astype(F32)
    bm_q = bm.astype(BF16).astype(F32)

    y_diag = jnp.zeros((R, SSM_INNER), F32)
    for k in range(n_tok):
        b_k = bm_q if k == 0 else pltpu.roll(bm_q, k, 0)
        prod = cm_q * b_k
        cb = jnp.zeros((R, LANES), F32)
        for g in reversed(range(SSM_GROUPS)):
            s_g = jnp.sum(prod[:, g * SSM_STATE:(g + 1) * SSM_STATE], axis=-1, keepdims=True)
            cb = jnp.where(lane_h < SSM_HEADS_PER_GROUP * (g + 1), s_g, cb)
        a_k = acum if k == 0 else pltpu.roll(acum, k, 0)
        decay = jnp.exp(jnp.where(t_idx >= k, acum - a_k, -jnp.inf))
        coef = (cb * decay).astype(BF16).astype(F32)
        x_k = xdt_q if k == 0 else pltpu.roll(xdt_q, k, 0)
        y_diag = y_diag + _expand_heads(coef, lane_g) * x_k

    xw_t = (xdt * _expand_heads(jnp.exp(rest), lane_g)).T
    total_t = jnp.exp(acum + rest).T
    bm_b = bm.astype(BF16)
    col = lax.broadcasted_iota(jnp.int32, (SSM_GROUP_DIM, R), 1)
    for b in range(SAMPLE_SEQ_TILE):
        r0 = b * SEQ_ROWS
        c_rows = cm[r0:r0 + SEQ_ROWS, :]
        in_seq = (col >= r0) & (col < r0 + SEQ_ROWS)
        last_col = r0 + SEQ_ROWS - 1
        for g in range(SSM_GROUPS):
            rows = slice(g * SSM_GROUP_DIM, (g + 1) * SSM_GROUP_DIM)
            h0 = st_ref[b, rows, :]
            c_g = c_rows[:, g * SSM_STATE:(g + 1) * SSM_STATE].astype(BF16)
            yoff_ref[r0:r0 + SEQ_ROWS, rows] = _dot_nt(c_g, h0.astype(BF16))
            lhs = jnp.where(in_seq, xw_t[rows, :], 0.0).astype(BF16)
            st = _dot(lhs, bm_b[:, g * SSM_STATE:(g + 1) * SSM_STATE])
            for j in range(SSM_HEADS_PER_GROUP):
                hh = SSM_HEADS_PER_GROUP * g + j
                hr = slice(j * SSM_HEAD_DIM, (j + 1) * SSM_HEAD_DIM)
                stn_ref[b, g * SSM_GROUP_DIM + j * SSM_HEAD_DIM:g * SSM_GROUP_DIM + (j + 1) * SSM_HEAD_DIM, :] = (
                    h0[hr, :] * total_t[hh:hh + 1, last_col:last_col + 1] + st[hr, :])

    y_ref[...] = (y_diag + yoff_ref[...] * _expand_heads(jnp.exp(acum), lane_g) + dskip_ref[...] * xs)


def _sample_ssd_chained_kernel(cext_ref, pext_ref, dtext_ref, st_ref, convw_ref, convb_ref, dtb_ref, alog_ref,
                              dskip_ref, poolw_ref, pscale_ref, prev_ref, *rest):
    del prev_ref
    _sample_ssd_kernel(cext_ref, pext_ref, dtext_ref, st_ref, convw_ref, convb_ref, dtb_ref, alog_ref,
                       dskip_ref, poolw_ref, pscale_ref, *rest)


def _sample_ssd_call(layer, cext, pext, dtext, state, new_state, p):
    n_seq = state.shape[1]
    R = SAMPLE_SEQ_TILE * SEQ_ROWS
    RP = SAMPLE_SEQ_TILE * POOL_SEQ_ROWS

    def vspec(rows, cols):
        return _const_spec((None, rows, cols), (layer, 0, 0))

    st_spec = pl.BlockSpec((None, SAMPLE_SEQ_TILE, SSM_INNER, SSM_STATE), lambda i: (layer, i, 0, 0))
    in_specs = [pl.BlockSpec((R, CONV_DIM), lambda i: (i, 0)),
                pl.BlockSpec((RP, POOL_DIM), lambda i: (i, 0)),
                pl.BlockSpec((R, LANES), lambda i: (i, 0)),
                st_spec,
                vspec(CONV_WIDTH, CONV_DIM), vspec(1, CONV_DIM), vspec(1, LANES), vspec(1, LANES),
                vspec(1, SSM_INNER),
                _const_spec((None, 4, POOL_GROUP_DIM, POOL_GROUP_DIM), (layer, 0, 0, 0)),
                vspec(1, POOL_DIM)]
    args = [cext, pext, dtext, state, p["conv_w"], p["conv_b"], p["dt_bias"], p["a_log"], p["d_skip"],
            p["pool_w"], p["pool_scale"]]
    kern, aliases = _sample_ssd_kernel, {}
    if new_state is not None:
        in_specs.append(pl.BlockSpec(memory_space=pl.ANY))
        args.append(new_state)
        kern, aliases = _sample_ssd_chained_kernel, {len(args) - 1: 2}
    return pl.pallas_call(
        kern,
        out_shape=(jax.ShapeDtypeStruct((n_seq * SEQ_ROWS, SSM_INNER), F32),
                   jax.ShapeDtypeStruct((n_seq * POOL_SEQ_ROWS, POOL_DIM), F32),
                   jax.ShapeDtypeStruct(state.shape, F32)),
        grid=(n_seq // SAMPLE_SEQ_TILE,),
        in_specs=in_specs,
        out_specs=(pl.BlockSpec((R, SSM_INNER), lambda i: (i, 0)),
                   pl.BlockSpec((RP, POOL_DIM), lambda i: (i, 0)),
                   st_spec),
        scratch_shapes=[pltpu.VMEM((R, SSM_INNER), F32)],
        input_output_aliases=aliases,
        compiler_params=pltpu.CompilerParams(dimension_semantics=("arbitrary",),
                                             vmem_limit_bytes=VMEM_LIMIT),
        name="sample_ssd",
    )(*args)


def _sample_attn_kernel(q_ref, k_ref, v_ref, o_ref, s_ref):
    tile, n_mem = k_ref.shape[0], k_ref.shape[1] // MEM_HEADS
    HR = MEM_HEADS * SEQ_ROWS
    row_head = lax.broadcasted_iota(jnp.int32, (HR, MEM_DIM), 0) // SEQ_ROWS
    col_head = lax.broadcasted_iota(jnp.int32, (HR, MEM_DIM), 1) // MEM_HEAD_DIM
    own_head = row_head == col_head

    def heads_on_lanes(ref, b):
        return jnp.concatenate([ref[b, pl.ds(h, n_mem, stride=MEM_HEADS), :] for h in range(MEM_HEADS)],
                               axis=1).astype(BF16)

    for b in range(tile):
        q = q_ref[b * SEQ_ROWS:(b + 1) * SEQ_ROWS, :]
        q_heads = jnp.where(own_head, jnp.concatenate([q] * MEM_HEADS, axis=0), 0.0).astype(BF16)
        s_ref[b * HR:(b + 1) * HR, :] = _dot_nt(q_heads, heads_on_lanes(k_ref, b))
    s = s_ref[...] * (MEM_HEAD_DIM ** -0.5)
    e = jnp.exp(s - jnp.max(s, axis=-1, keepdims=True))
    p = (e * (1.0 / jnp.sum(e, axis=-1, keepdims=True))).astype(BF16)
    for b in range(tile):
        o = _dot(p[b * HR:(b + 1) * HR, :], heads_on_lanes(v_ref, b))
        o_ref[b * SEQ_ROWS:(b + 1) * SEQ_ROWS, :] = jnp.concatenate(
            [o[h * SEQ_ROWS:(h + 1) * SEQ_ROWS, h * MEM_HEAD_DIM:(h + 1) * MEM_HEAD_DIM]
             for h in range(MEM_HEADS)], axis=1)


def _sample_merge_kernel(x_ref, z_ref, g_ref, y_ref, ypool_ref, ymem_ref, gbias_ref, snorm_ref,
                         wbp_ref, wbs_ref, wbm_ref, wo_ref, xo_ref):
    xo_ref[...] = _merge_tail(x_ref[...], _silu(z_ref[...]), jax.nn.sigmoid(g_ref[...] + gbias_ref[...]),
                              y_ref[...], ypool_ref[...], ymem_ref[...], snorm_ref[...],
                              wbp_ref, wbs_ref, wbm_ref, wo_ref)


def _sample_merge_call(x, layer, row_block, n_rows, z, graw, y_raw, y_pool, y_mem, p):
    def full(cols):
        return pl.BlockSpec((n_rows, cols), lambda i: (0, 0))

    def wspec(rows, cols):
        return _const_spec((None, rows, cols), (layer, 0, 0))

    x_spec = pl.BlockSpec((n_rows, D_MODEL), lambda i: (row_block, 0))
    return pl.pallas_call(
        _sample_merge_kernel,
        out_shape=jax.ShapeDtypeStruct(x.shape, F32),
        grid=(1,),
        in_specs=[x_spec, full(SSM_INNER), full(N_BRANCH * D_MODEL), full(SSM_INNER), full(POOL_DIM),
                  full(MEM_DIM), wspec(1, N_BRANCH * D_MODEL), wspec(1, SSM_INNER),
                  wspec(POOL_DIM, D_MODEL), wspec(SSM_INNER, D_MODEL), wspec(MEM_DIM, D_MODEL),
                  wspec(D_MODEL, D_MODEL)],
        out_specs=x_spec,
        input_output_aliases={0: 0},
        compiler_params=pltpu.CompilerParams(dimension_semantics=("arbitrary",),
                                             vmem_limit_bytes=VMEM_LIMIT),
        name="sample_merge",
    )(x, z, graw, y_raw, y_pool, y_mem, p["gate_bias"], p["ssm_norm"],
      p["w_br_pool"], p["w_br_ssm"], p["w_br_mem"], p["w_o"])


def _row3(a):
    return a.reshape(a.shape[0], 1, a.shape[1])


def _pad_lanes(a):
    return jnp.pad(a, [(0, 0)] * (a.ndim - 1) + [(0, LANES - a.shape[-1])])


def _token_rows(ext, rows_per_seq, n_tok):
    n_seq = ext.shape[0] // rows_per_seq
    return ext.reshape(n_seq, rows_per_seq, -1)[:, rows_per_seq - n_tok:, :].reshape(n_seq * n_tok, -1)


def kernel(x_prompt, x_sample, mem_prompt, state_pool, state_conv, state_ssm, cache_mem_k, cache_mem_v,
           ffn1_norm, ffn1_w_gate, ffn1_w_up, ffn1_w_down, mix_norm, w_in, gate_bias,
           pool_w, pool_scale, conv_w, conv_b, dt_bias, a_log, d_skip, ssm_norm,
           mem_norm, w_mem_k, w_mem_v, w_br_pool, w_br_ssm, w_br_mem, w_o,
           ffn2_norm, ffn2_w_gate, ffn2_w_up, ffn2_w_down, final_norm):
    depth = w_in.shape[0]
    batch, seq, _ = x_prompt.shape
    n_seq, n_tok, _ = x_sample.shape
    n_mem = mem_prompt.shape[1]
    n_prompt = batch * seq
    n_sample = n_seq * n_tok
    assert n_sample == FFN_TOKEN_TILE and n_prompt % FFN_TOKEN_TILE == 0 and seq % PROMPT_CHUNK == 0
    assert n_tok == SEQ_ROWS // 2 and PAST_LEN >= max(POOL_WINDOWS)
    assert n_seq % SAMPLE_SEQ_TILE == 0
    sample_block = n_prompt // n_sample

    w_pool, w_z, w_xbc, w_dt, w_q, w_g = _split_w_in_call(w_in)
    params = dict(
        mix_norm=_row3(mix_norm),
        w_pool=w_pool, w_z=w_z, w_xbc=w_xbc, w_dt=w_dt, w_q=w_q, w_g=w_g,
        gate_bias=_row3(gate_bias), pool_w=pool_w.astype(BF16), pool_scale=_row3(pool_scale),
        conv_w=conv_w, conv_b=_row3(conv_b), dt_bias=_row3(_pad_lanes(dt_bias)), a_log=_row3(_pad_lanes(a_log)),
        d_skip=_row3(jnp.repeat(d_skip, SSM_HEAD_DIM, axis=1)), ssm_norm=_row3(ssm_norm),
        w_br_pool=w_br_pool.astype(BF16), w_br_ssm=w_br_ssm.astype(BF16), w_br_mem=w_br_mem.astype(BF16),
        w_o=w_o.astype(BF16))
    ffn1 = (_row3(ffn1_norm), ffn1_w_gate.astype(BF16), ffn1_w_up.astype(BF16), ffn1_w_down.astype(BF16))
    ffn2 = (_row3(ffn2_norm), ffn2_w_gate.astype(BF16), ffn2_w_up.astype(BF16), ffn2_w_down.astype(BF16))

    mem_k, mem_v, mem_kb, mem_vb = _memkv_call(mem_prompt, _row3(mem_norm), w_mem_k.astype(BF16),
                                               w_mem_v.astype(BF16))
    cache_k = cache_mem_k.reshape(depth, n_seq, n_mem * MEM_HEADS, MEM_HEAD_DIM)
    cache_v = cache_mem_v.reshape(depth, n_seq, n_mem * MEM_HEADS, MEM_HEAD_DIM)
    state_ssm_rows = state_ssm.reshape(depth, n_seq, SSM_INNER, SSM_STATE)

    pool_p, conv_p, ssm_p, pool_s, conv_s = [], [], [], [], []
    ssm_s = None
    for l in range(depth):
        if l == 0:
            x = _ffn_first_call(x_prompt.reshape(n_prompt, D_MODEL), x_sample.reshape(n_sample, D_MODEL), l, ffn1)
        else:
            x = _ffn_call(x, l, ffn1)
        def pad_seq(a):
            a3 = a.reshape(n_seq, n_tok, a.shape[-1])
            return jnp.concatenate([jnp.zeros_like(a3), a3], axis=1).reshape(n_seq * SEQ_ROWS, a.shape[-1])

        u, z, xbc, dt_raw, q, graw = _sample_inproj_call(x, l, sample_block, n_sample, params)
        x, npool, nconv, nssm, ymem_ext = _prompt_mixer_call(x, l, batch, seq, params, mem_kb, mem_vb,
                                                             pad_seq(q), cache_k, cache_v)
        pool_p.append(npool)
        conv_p.append(nconv)
        ssm_p.append(nssm.reshape(batch, SSM_HEADS, SSM_HEAD_DIM, SSM_STATE))

        u3 = u.reshape(n_seq, n_tok, POOL_DIM)
        xbc3 = xbc.reshape(n_seq, n_tok, CONV_DIM)
        pext = jnp.concatenate([jnp.zeros((n_seq, POOL_SEQ_ROWS - POOL_BUF - n_tok, POOL_DIM), F32),
                                state_pool[l], u3], axis=1).reshape(n_seq * POOL_SEQ_ROWS, POOL_DIM)
        cext = jnp.concatenate([jnp.zeros((n_seq, SEQ_ROWS - CONV_WIDTH + 1 - n_tok, CONV_DIM), F32),
                                state_conv[l], xbc3], axis=1).reshape(n_seq * SEQ_ROWS, CONV_DIM)
        y_ext, ypool_ext, ssm_s = _sample_ssd_call(l, cext, pext, pad_seq(dt_raw), state_ssm_rows, ssm_s, params)
        x = _sample_merge_call(x, l, sample_block, n_sample, z, graw,
                               _token_rows(y_ext, SEQ_ROWS, n_tok), _token_rows(ypool_ext, POOL_SEQ_ROWS, n_tok),
                               _token_rows(ymem_ext, SEQ_ROWS, n_tok), params)
        pool_s.append(jnp.concatenate([state_pool[l][:, n_tok:], u3], axis=1))
        conv_s.append(xbc3[:, n_tok - (CONV_WIDTH - 1):])

        if l < depth - 1:
            x = _ffn_call(x, l, ffn2)
    y_prompt, y_sample = _ffn_final_call(x, n_prompt, depth - 1, ffn2, final_norm.reshape(1, D_MODEL))

    mem_shape = (depth, batch, n_mem, MEM_HEADS, MEM_HEAD_DIM)
    return (y_prompt.reshape(batch, seq, D_MODEL), y_sample.reshape(n_seq, n_tok, D_MODEL),
            jnp.stack(pool_p), jnp.stack(conv_p), jnp.stack(ssm_p),
            mem_k.reshape(mem_shape), mem_v.reshape(mem_shape),
            jnp.stack(pool_s), jnp.stack(conv_s),
            ssm_s.reshape(depth, n_seq, SSM_HEADS, SSM_HEAD_DIM, SSM_STATE))
```
